```python
import math
import jax, jax.numpy as jnp
from jax import lax
import numpy as np

D_MODEL = 1024
BATCH = 32
SEQ = 256
DEPTH = 4
DEC_BATCH = 4
DEC_SEQ = 4096
PAST_LEN = 512

GRID_W = 64
N_EVEN = (DEPTH + 1) // 2
N_ODD = DEPTH // 2
HEAD_DIM = 64
ATTN_SCALE = HEAD_DIM ** -0.5
ROPE_BASE = 10000.0
RMS_EPS = 1e-6
N_MOD = 6
A_HEADS = 8
A_KV_HEADS = 2
A_GROUP = A_HEADS // A_KV_HEADS
A_WINDOW = 128
A_BLOCK = 128
A_WIDTH = A_HEADS * HEAD_DIM
B_HEADS = 4
B_DK = 64
B_DV = 128
B_GATE_RANK = 16
B_GATE_TAU = 16.0
B_CHUNK = 64
B_WIDTH = B_HEADS * B_DV
C_HEADS = 8
C_DV = 2 * HEAD_DIM
C_BLOCK = 128
C_WIDTH = C_HEADS * C_DV
EVEN_SPLITS = (A_HEADS * HEAD_DIM, A_KV_HEADS * HEAD_DIM, A_KV_HEADS * HEAD_DIM, B_HEADS * B_DK, B_HEADS * B_DK, B_WIDTH, B_WIDTH, 2 * B_GATE_RANK)
EVEN_IN = sum(EVEN_SPLITS)
EVEN_OUT = A_WIDTH + B_WIDTH
ODD_IN = 3 * C_WIDTH
P_HEADS = 8
P_NKEYS = 128
P_EXPERTS = P_NKEYS * P_NKEYS
P_DKEY = 128
P_TOPK = 16
P_BLOCK = 128

kernel_name = 'hybrid_diffusion_prefix_trunk_step'

F32 = jnp.float32


def rmsnorm(x, g):
    xf = x.astype(F32)
    y = xf * lax.rsqrt(jnp.mean(xf * xf, axis=-1, keepdims=True) + RMS_EPS)
    return (y * g.astype(F32)).astype(x.dtype)


def modulation(cvec, w, b):
    m = (jax.nn.silu(cvec) @ w + b)[:, None, :]
    return jnp.split(m, N_MOD, axis=-1)


def rope_2d(T):
    rows = T // GRID_W
    t = jnp.arange(rows * GRID_W)
    row = (t // GRID_W).astype(F32)
    col = (t % GRID_W).astype(F32)
    nf = HEAD_DIM // 4
    freqs = ROPE_BASE ** (-jnp.arange(nf, dtype=F32) / nf)
    ang = jnp.stack([row[:, None] * freqs, col[:, None] * freqs], axis=1)
    return jnp.cos(ang), jnp.sin(ang)


def apply_rope_2d(x, cos, sin):
    extra = x.ndim - 3
    shp = (cos.shape[0],) + (1,) * extra + cos.shape[1:]
    cs = cos.reshape(shp)
    sn = sin.reshape(shp)
    xr = x.astype(F32).reshape(x.shape[:-1] + (2, 2, HEAD_DIM // 4))
    x1 = xr[..., 0, :]
    x2 = xr[..., 1, :]
    out = jnp.stack([x1 * cs - x2 * sn, x2 * cs + x1 * sn], axis=-2)
    return out.reshape(x.shape).astype(x.dtype)


def attn_sink_dense(q, k, v, sink):
    B, Tq = q.shape[:2]
    s = jnp.einsum('bqkgd,bckd->bkgqc', q, k).astype(F32) * ATTN_SCALE
    sk = jnp.broadcast_to(sink.astype(F32)[None, :, :, None, None], s.shape[:-1] + (1,))
    p = jax.nn.softmax(jnp.concatenate([s, sk], axis=-1), axis=-1)[..., :-1]
    out = jnp.einsum('bkgqc,bckd->bqkgd', p.astype(v.dtype), v)
    return out.reshape(B, Tq, A_WIDTH)


def attn_window_ctx(q, k, v, kc, vc, sink):
    B, T = q.shape[:2]
    W = A_BLOCK
    NB = T // W
    Lc = kc.shape[1]
    qb = q.reshape(B, NB, W, A_KV_HEADS, A_GROUP, HEAD_DIM)

    def band(x):
        xp = jnp.pad(x, ((0, 0), (W, W), (0, 0), (0, 0))).reshape(B, NB + 2, W, A_KV_HEADS, HEAD_DIM)
        return jnp.concatenate([xp[:, :-2], xp[:, 1:-1], xp[:, 2:]], axis=2)

    kw = band(k)
    vw = band(v)
    s_w = jnp.einsum('bnikgd,bnjkd->bnkgij', qb, kw).astype(F32) * ATTN_SCALE
    blk = jnp.arange(NB)[:, None, None] * W
    qpos = blk + jnp.arange(W)[None, :, None]
    kpos = blk - W + jnp.arange(3 * W)[None, None, :]
    valid = (jnp.abs(qpos - kpos) <= A_WINDOW) & (kpos >= 0) & (kpos < T)
    s_w = jnp.where(valid[None, :, None, None], s_w, -jnp.inf)
    s_c = jnp.einsum('bnikgd,bckd->bnkgic', qb, kc).astype(F32) * ATTN_SCALE
    sk = jnp.broadcast_to(sink.astype(F32)[None, None, :, :, None, None], s_c.shape[:-1] + (1,))
    p = jax.nn.softmax(jnp.concatenate([s_w, s_c, sk], axis=-1), axis=-1)
    pw = p[..., :3 * W].astype(v.dtype)
    pc = p[..., 3 * W:3 * W + Lc].astype(v.dtype)
    out = jnp.einsum('bnkgij,bnjkd->bnikgd', pw, vw) + jnp.einsum('bnkgic,bckd->bnikgd', pc, vc)
    return out.reshape(B, T, A_WIDTH)


def gla_chunked(q, k, v, lg, s0):
    B, H, T, dk = q.shape
    dv = v.shape[-1]
    L = B_CHUNK
    N = T // L
    q = q.reshape(B, H, N, L, dk)
    k = k.reshape(B, H, N, L, dk)
    v = v.reshape(B, H, N, L, dv)
    b = jnp.cumsum(lg.reshape(B, H, N, L, dk), axis=3)
    b_last = b[:, :, :, -1:, :]
    qd = q * jnp.exp(b)
    kd = k * jnp.exp(-b)
    kt = k * jnp.exp(b_last - b)
    causal = jnp.tril(jnp.ones((L, L), dtype=bool))
    a = jnp.where(causal, jnp.einsum('bhnid,bhnjd->bhnij', qd, kd), 0.0)
    o_intra = jnp.einsum('bhnij,bhnjv->bhniv', a, v)
    kv = jnp.einsum('bhnjd,bhnjv->nbhdv', kt, v)
    decay = jnp.moveaxis(jnp.exp(b_last[:, :, :, 0, :]), 2, 0)

    def step(s, inp):
        dec, kvn = inp
        return dec[..., None] * s + kvn, s

    s_fin, s_starts = lax.scan(step, s0, (decay, kv))
    o_inter = jnp.einsum('bhnid,nbhdv->bhniv', qd, s_starts)
    return (o_intra + o_inter).reshape(B, H, T, dv), s_fin


def gla_bidir(q, k, v, lg_f, lg_b, s0_f, s0_b):
    def tr(x):
        return jnp.swapaxes(x, 1, 2).astype(F32)

    def fl(x):
        return jnp.flip(x, axis=2)

    qh = tr(q) * (B_DK ** -0.5)
    kh = tr(k)
    vh = tr(v)
    of, sf = gla_chunked(qh, kh, vh, tr(lg_f), s0_f.astype(F32))
    ob, sb = gla_chunked(fl(qh), fl(kh), fl(vh), fl(tr(lg_b)), s0_b.astype(F32))
    o = of + fl(ob)
    return jnp.swapaxes(o, 1, 2), sf, sb


def even_project(h, w_in, a_qn, a_kn, gw_f, gb_f, gw_b, gb_b):
    B, T, _ = h.shape
    offs = [int(o) for o in np.cumsum(EVEN_SPLITS)[:-1]]
    aq, ak, av, bq, bk, bv, br, bg = jnp.split(h @ w_in, offs, axis=-1)
    aq = rmsnorm(aq.reshape(B, T, A_KV_HEADS, A_GROUP, HEAD_DIM), a_qn)
    ak = rmsnorm(ak.reshape(B, T, A_KV_HEADS, HEAD_DIM), a_kn)
    av = av.reshape(B, T, A_KV_HEADS, HEAD_DIM)
    bq = bq.reshape(B, T, B_HEADS, B_DK)
    bk = bk.reshape(B, T, B_HEADS, B_DK)
    bv = bv.reshape(B, T, B_HEADS, B_DV)
    bgf, bgb = jnp.split(bg, 2, axis=-1)
    lg_f = (jax.nn.log_sigmoid((bgf @ gw_f + gb_f).astype(F32)) / B_GATE_TAU).reshape(B, T, B_HEADS, B_DK)
    lg_b = (jax.nn.log_sigmoid((bgb @ gw_b + gb_b).astype(F32)) / B_GATE_TAU).reshape(B, T, B_HEADS, B_DK)
    return aq, ak, av, bq, bk, bv, br, lg_f, lg_b


def even_output(oa, ob, br, b_on, w_out):
    B, T, _ = oa.shape
    ob = rmsnorm(ob.astype(br.dtype), b_on).reshape(B, T, B_WIDTH) * jax.nn.silu(br)
    return jnp.concatenate([oa, ob.astype(oa.dtype)], axis=-1) @ w_out


def odd_project(h, w_in, q_norm, k_norm):
    B, T, _ = h.shape
    q, k, v = jnp.split(h @ w_in, 3, axis=-1)
    q = rmsnorm(q.reshape(B, T, C_HEADS, 2, HEAD_DIM), q_norm)
    k = rmsnorm(k.reshape(B, T, C_HEADS, 2, HEAD_DIM), k_norm)
    return q, k, v.reshape(B, T, C_HEADS, C_DV)


def diff_lambda(lq1, lk1, lq2, lk2, lam_init):
    return (jnp.exp(jnp.sum(lq1.astype(F32) * lk1.astype(F32))) - jnp.exp(jnp.sum(lq2.astype(F32) * lk2.astype(F32))) + lam_init)


def diff_attention(q, k, v, lam):
    B, Tq = q.shape[:2]
    NB = Tq // C_BLOCK
    qb = jnp.moveaxis(q.reshape(B, NB, C_BLOCK, C_HEADS, 2, HEAD_DIM), 1, 0)

    def block(qi):
        s = jnp.einsum('bihmd,bjhmd->bhmij', qi, k).astype(F32) * ATTN_SCALE
        p = jax.nn.softmax(s, axis=-1)
        pd = p[:, :, 0] - lam * p[:, :, 1]
        return jnp.einsum('bhij,bjhv->bihv', pd.astype(v.dtype), v)

    o = lax.map(block, qb)
    return jnp.moveaxis(o, 0, 1).reshape(B, Tq, C_HEADS, C_DV)


def odd_output(o, c_on, lam_init, w_out):
    B, T = o.shape[:2]
    o = rmsnorm(o, c_on) * (1.0 - lam_init)
    return o.reshape(B, T, C_WIDTH) @ w_out


def lambda_init(layer):
    return 0.8 - 0.6 * math.exp(-0.3 * layer)


def peer(h, wq, sub_keys, u, v):
    B, T, D = h.shape
    xs = h.reshape(-1, P_BLOCK, D)

    def block(xc):
        q = (xc @ wq).reshape(P_BLOCK, P_HEADS, 2, P_DKEY // 2)
        s = jnp.einsum('chpd,hpkd->chpk', q, sub_keys).astype(F32)
        sv, si = lax.top_k(s, P_TOPK)
        cand = (sv[:, :, 0, :, None] + sv[:, :, 1, None, :]).reshape(P_BLOCK, P_HEADS, P_TOPK * P_TOPK)
        cidx = (si[:, :, 0, :, None] * P_NKEYS + si[:, :, 1, None, :]).reshape(P_BLOCK, P_HEADS, P_TOPK * P_TOPK)
        fs, fi = lax.top_k(cand, P_TOPK)
        eidx = jnp.take_along_axis(cidx, fi, axis=-1)
        g = jax.nn.softmax(fs, axis=-1)
        act = jax.nn.gelu(jnp.einsum('cd,chkd->chk', xc, u[eidx]).astype(F32), approximate=False)
        return jnp.einsum('chk,chkd->cd', (g * act).astype(v.dtype), v[eidx])

    return lax.map(block, xs).reshape(B, T, D)


def setup_inputs(seed: int = 0) -> dict:
    key = jax.random.key(seed)
    ks = iter(jax.random.split(key, 64))
    D = D_MODEL

    def nrm(shape, scale):
        return jax.random.normal(next(ks), shape, jnp.float32) * scale

    def gain(shape):
        return 1.0 + nrm(shape, 0.1)

    return {
        'x_prompt': nrm((BATCH, SEQ, D), 1.0),
        'x_sample': nrm((DEC_BATCH, DEC_SEQ, D), 1.0),
        'cache_a_k': nrm((DEC_BATCH, N_EVEN, PAST_LEN, A_KV_HEADS, HEAD_DIM), 1.0),
        'cache_a_v': nrm((DEC_BATCH, N_EVEN, PAST_LEN, A_KV_HEADS, HEAD_DIM), 1.0),
        'state_b_fwd': nrm((DEC_BATCH, N_EVEN, B_HEADS, B_DK, B_DV), 0.5),
        'state_b_bwd': nrm((DEC_BATCH, N_EVEN, B_HEADS, B_DK, B_DV), 0.5),
        'cache_c_k': nrm((DEC_BATCH, N_ODD, PAST_LEN, C_HEADS, 2, HEAD_DIM), 1.0),
        'cache_c_v': nrm((DEC_BATCH, N_ODD, PAST_LEN, C_HEADS, C_DV), 1.0),
        'c': nrm((DEC_BATCH, D), 1.0),
        'c_ctx': nrm((D,), 1.0),
        'ada_w': nrm((DEPTH, D, N_MOD * D), 0.5 * D ** -0.5),
        'ada_b': nrm((DEPTH, N_MOD * D), 0.02),
        'norm_mix_g': gain((DEPTH, D)),
        'norm_ffn_g': gain((DEPTH, D)),
        'e_w_in': nrm((N_EVEN, D, EVEN_IN), D ** -0.5),
        'e_w_out': nrm((N_EVEN, EVEN_OUT, D), EVEN_OUT ** -0.5),
        'a_q_norm': gain((N_EVEN, HEAD_DIM)),
        'a_k_norm': gain((N_EVEN, HEAD_DIM)),
        'a_sink': nrm((N_EVEN, A_KV_HEADS, A_GROUP), 0.5),
        'b_gate_w_f': nrm((N_EVEN, B_GATE_RANK, B_HEADS * B_DK), B_GATE_RANK ** -0.5),
        'b_gate_b_f': nrm((N_EVEN, B_HEADS * B_DK), 0.1),
        'b_gate_w_b': nrm((N_EVEN, B_GATE_RANK, B_HEADS * B_DK), B_GATE_RANK ** -0.5),
        'b_gate_b_b': nrm((N_EVEN, B_HEADS * B_DK), 0.1),
        'b_out_norm': gain((N_EVEN, B_DV)),
        'o_w_in': nrm((N_ODD, D, ODD_IN), D ** -0.5),
        'o_w_out': nrm((N_ODD, C_WIDTH, D), C_WIDTH ** -0.5),
        'c_q_norm': gain((N_ODD, HEAD_DIM)),
        'c_k_norm': gain((N_ODD, HEAD_DIM)),
        'c_lambda_q1': nrm((N_ODD, HEAD_DIM), 0.1),
        'c_lambda_k1': nrm((N_ODD, HEAD_DIM), 0.1),
        'c_lambda_q2': nrm((N_ODD, HEAD_DIM), 0.1),
        'c_lambda_k2': nrm((N_ODD, HEAD_DIM), 0.1),
        'c_out_norm': gain((N_ODD, C_DV)),
        'p_w_q': nrm((DEPTH, D, P_HEADS * P_DKEY), D ** -0.5),
        'p_sub_keys': nrm((DEPTH, P_HEADS, 2, P_NKEYS, P_DKEY // 2), (P_DKEY // 2) ** -0.5),
        'p_u': nrm((DEPTH, P_EXPERTS, D), D ** -0.5),
        'p_v': nrm((DEPTH, P_EXPERTS, D), 0.25),
    }


def reference(x_prompt, x_sample, cache_a_k, cache_a_v, state_b_fwd, state_b_bwd, cache_c_k, cache_c_v, c, c_ctx, ada_w, ada_b, norm_mix_g, norm_ffn_g, e_w_in, e_w_out, a_q_norm, a_k_norm, a_sink, b_gate_w_f, b_gate_b_f, b_gate_w_b, b_gate_b_b, b_out_norm, o_w_in, o_w_out, c_q_norm, c_k_norm, c_lambda_q1, c_lambda_k1, c_lambda_q2, c_lambda_k2, c_out_norm, p_w_q, p_sub_keys, p_u, p_v):
    cos, sin = rope_2d(x_sample.shape[1])

    xp = x_prompt
    Bp = xp.shape[0]
    ak_l, av_l, sf_l, sb_l, ck_l, cv_l = [], [], [], [], [], []
    for l in range(DEPTH):
        i = l // 2
        sh1, sc1, g1, sh2, sc2, g2 = modulation(c_ctx[None, :], ada_w[l], ada_b[l])
        h = rmsnorm(xp, norm_mix_g[l]) * (1.0 + sc1) + sh1
        if l % 2 == 0:
            aq, ak, av, bq, bk, bv, br, lgf, lgb = even_project(h, e_w_in[i], a_q_norm[i], a_k_norm[i], b_gate_w_f[i], b_gate_b_f[i], b_gate_w_b[i], b_gate_b_b[i])
            oa = attn_sink_dense(aq, ak, av, a_sink[i])
            s0 = jnp.zeros((Bp, B_HEADS, B_DK, B_DV), F32)
            ob, sf, sb = gla_bidir(bq, bk, bv, lgf, lgb, s0, s0)
            o = even_output(oa, ob, br, b_out_norm[i], e_w_out[i])
            ak_l.append(ak)
            av_l.append(av)
            sf_l.append(sf)
            sb_l.append(sb)
        else:
            lam_init = lambda_init(l)
            lam = diff_lambda(c_lambda_q1[i], c_lambda_k1[i], c_lambda_q2[i], c_lambda_k2[i], lam_init)
            q, k, v = odd_project(h, o_w_in[i], c_q_norm[i], c_k_norm[i])
            o = odd_output(diff_attention(q, k, v, lam), c_out_norm[i], lam_init, o_w_out[i])
            ck_l.append(k)
            cv_l.append(v)
        xp = xp + g1 * o
        h = rmsnorm(xp, norm_ffn_g[l]) * (1.0 + sc2) + sh2
        xp = xp + g2 * peer(h, p_w_q[l], p_sub_keys[l], p_u[l], p_v[l])

    xs = x_sample
    for l in range(DEPTH):
        i = l // 2
        sh1, sc1, g1, sh2, sc2, g2 = modulation(c, ada_w[l], ada_b[l])
        h = rmsnorm(xs, norm_mix_g[l]) * (1.0 + sc1) + sh1
        if l % 2 == 0:
            aq, ak, av, bq, bk, bv, br, lgf, lgb = even_project(h, e_w_in[i], a_q_norm[i], a_k_norm[i], b_gate_w_f[i], b_gate_b_f[i], b_gate_w_b[i], b_gate_b_b[i])
            aq = apply_rope_2d(aq, cos, sin)
            ak = apply_rope_2d(ak, cos, sin)
            oa = attn_window_ctx(aq, ak, av, cache_a_k[:, i].astype(ak.dtype), cache_a_v[:, i].astype(av.dtype), a_sink[i])
            ob, _, _ = gla_bidir(bq, bk, bv, lgf, lgb, state_b_fwd[:, i], state_b_bwd[:, i])
            o = even_output(oa, ob, br, b_out_norm[i], e_w_out[i])
        else:
            lam_init = lambda_init(l)
            lam = diff_lambda(c_lambda_q1[i], c_lambda_k1[i], c_lambda_q2[i], c_lambda_k2[i], lam_init)
            q, k, v = odd_project(h, o_w_in[i], c_q_norm[i], c_k_norm[i])
            q = apply_rope_2d(q, cos, sin)
            k = apply_rope_2d(k, cos, sin)
            k_all = jnp.concatenate([k, cache_c_k[:, i].astype(k.dtype)], axis=1)
            v_all = jnp.concatenate([v, cache_c_v[:, i].astype(v.dtype)], axis=1)
            o = odd_output(diff_attention(q, k_all, v_all, lam), c_out_norm[i], lam_init, o_w_out[i])
        xs = xs + g1 * o
        h = rmsnorm(xs, norm_ffn_g[l]) * (1.0 + sc2) + sh2
        xs = xs + g2 * peer(h, p_w_q[l], p_sub_keys[l], p_u[l], p_v[l])

    new_cache_a_k = jnp.stack(ak_l, axis=1)
    new_cache_a_v = jnp.stack(av_l, axis=1)
    new_state_b_fwd = jnp.stack(sf_l, axis=1)
    new_state_b_bwd = jnp.stack(sb_l, axis=1)
    new_cache_c_k = jnp.stack(ck_l, axis=1)
    new_cache_c_v = jnp.stack(cv_l, axis=1)
    return (xp, xs, new_cache_a_k, new_cache_a_v, new_state_b_fwd, new_state_b_bwd, new_cache_c_k, new_cache_c_v)
```

```python
import functools
import math

import jax
import jax.numpy as jnp
from jax import lax
from jax.experimental import pallas as pl
from jax.experimental.pallas import tpu as pltpu

F32 = jnp.float32
BF16 = jnp.bfloat16
HI = lax.Precision.HIGHEST

D_MODEL = 1024
GRID_W = 64
HEAD_DIM = 64
ATTN_SCALE = HEAD_DIM ** -0.5
ROPE_BASE = 10000.0
RMS_EPS = 1e-6
N_MOD = 6
A_HEADS = 8
A_KV_HEADS = 2
A_WINDOW = 128
B_HEADS = 4
B_DK = 64
B_DV = 128
B_GATE_RANK = 16
B_GATE_TAU = 16.0
B_CHUNK = 64
C_HEADS = 8
C_DV = 128
P_HEADS = 8
P_NKEYS = 128
P_TOPK = 16

LANE = 128
SUBLANES = 8
ROW_BLOCK = 256
LIN_BLOCK = 512
PEER_TOKENS = 512
PEER_EXPERTS = SUBLANES * P_NKEYS
ATTN_Q_BLOCK = 128
NEG_BIG = -1e30
VMEM_LIMIT = 56 * 1024 * 1024

E_AQ, E_AK, E_AV, E_BQ, E_BV, E_BR, E_BK, E_BG, E_WIDTH = 0, 512, 640, 768, 1024, 1536, 2048, 2304, 2432
E_QK_WIDTH = 640
O_QK_WIDTH = 2048


def _dot(a, b):
    return lax.dot_general(a.astype(BF16), b.astype(BF16), (((1,), (0,)), ((), ())),
                           preferred_element_type=F32)


def _dot_nt(a, b):
    return lax.dot_general(a.astype(BF16), b.astype(BF16), (((1,), (1,)), ((), ())),
                           preferred_element_type=F32)


def _dot_hi(a, b):
    return lax.dot_general(a, b, (((1,), (0,)), ((), ())), precision=HI, preferred_element_type=F32)


def _params(n_axes):
    return pltpu.CompilerParams(dimension_semantics=("arbitrary",) * n_axes,
                                vmem_limit_bytes=VMEM_LIMIT)


def _lo_lanes(shape):
    return (lax.broadcasted_iota(jnp.int32, shape, len(shape) - 1) & 64) == 0


def _mod_body(c_ref, w_ref, b_ref, o_ref):
    c = c_ref[...]
    s = c * jax.nn.sigmoid(c)
    o_ref[0] = _dot_hi(s, w_ref[0]) + b_ref[0]


def _modulation(cvecs, ada_w, ada_b):
    depth, d, n = ada_w.shape
    r = cvecs.shape[0]
    tn = 1536
    return pl.pallas_call(
        _mod_body,
        grid=(depth, n // tn),
        in_specs=[pl.BlockSpec((r, d), lambda l, j: (0, 0)),
                  pl.BlockSpec((1, d, tn), lambda l, j: (l, 0, j)),
                  pl.BlockSpec((1, 1, tn), lambda l, j: (l, 0, j))],
        out_specs=pl.BlockSpec((1, r, tn), lambda l, j: (l, 0, j)),
        out_shape=jax.ShapeDtypeStruct((depth, r, n), F32),
        compiler_params=_params(2),
        name="modulation",
    )(cvecs, ada_w, ada_b.reshape(depth, 1, n))


class _Layout:
    def __init__(self, bc, tc, bl, tl):
        self.bc, self.tc, self.bl, self.tl = bc, tc, bl, tl
        self.nctx = bc * tc
        self.nt = self.nctx + bl * tl
        assert tc == ROW_BLOCK and tl % LIN_BLOCK == 0 and self.nctx % LIN_BLOCK == 0
        assert self.nctx % tl == 0 and tl % GRID_W == 0 and self.nt % PEER_TOKENS == 0
        assert self.nctx % PEER_TOKENS == 0 and tl % PEER_TOKENS == 0

    def mod_row(self, i, block):
        nctx_blocks = self.nctx // block
        return jnp.where(i < nctx_blocks, self.bl, (i - nctx_blocks) // (self.tl // block))


def _mod_spec(lay, block, chunk):
    return pl.BlockSpec((1, 1, D_MODEL), lambda i, *_: (lay.mod_row(i, block), 0, chunk))


def _rms_mod(x, g, sc, sh):
    ms = jnp.mean(x * x, axis=-1, keepdims=True)
    return (x * lax.rsqrt(ms + RMS_EPS) * g) * (1.0 + sc) + sh


def _ln_proj_body(x_ref, g_ref, sc_ref, sh_ref, w_ref, o_ref):
    h = _rms_mod(x_ref[...], g_ref[...], sc_ref[0], sh_ref[0])
    o_ref[...] = _dot(h, w_ref[...])


def _ln_proj(lay, x, g, mod, chunk, w):
    n_out = w.shape[1]
    return pl.pallas_call(
        _ln_proj_body,
        grid=(lay.nt // LIN_BLOCK,),
        in_specs=[pl.BlockSpec((LIN_BLOCK, D_MODEL), lambda i: (i, 0)),
                  pl.BlockSpec((1, D_MODEL), lambda i: (0, 0)),
                  _mod_spec(lay, LIN_BLOCK, chunk + 1),
                  _mod_spec(lay, LIN_BLOCK, chunk),
                  pl.BlockSpec((D_MODEL, n_out), lambda i: (0, 0))],
        out_specs=pl.BlockSpec((LIN_BLOCK, n_out), lambda i: (i, 0)),
        out_shape=jax.ShapeDtypeStruct((lay.nt, n_out), F32),
        compiler_params=_params(1),
        name="ln_proj",
    )(x, g.reshape(1, D_MODEL), mod, mod, w)


def _prep_body(x_ref, g_ref, cos_ref, sin_ref, bd_ref, o_ref, *, n_chunks):
    c = cos_ref[...]
    s = sin_ref[...]
    first = (lax.broadcasted_iota(jnp.int32, c.shape, 1) & 16) == 0
    for j in range(n_chunks):
        cols = slice(j * LANE, (j + 1) * LANE)
        x = x_ref[:, cols]
        ms = _dot_hi(x * x, bd_ref[...])
        y = x * lax.rsqrt(ms + RMS_EPS) * g_ref[:, cols]
        partner = jnp.where(first, pltpu.roll(y, LANE - 16, 1), pltpu.roll(y, 16, 1))
        o_ref[:, cols] = y * c + partner * s


def _rope_tables(tl):
    t = jnp.arange(tl)
    row = (t // GRID_W).astype(F32)
    col = (t % GRID_W).astype(F32)
    nf = HEAD_DIM // 4
    freqs = ROPE_BASE ** (-jnp.arange(nf, dtype=F32) / nf)
    ang_r = row[:, None] * freqs
    ang_c = col[:, None] * freqs
    cos64 = jnp.concatenate([jnp.cos(ang_r), jnp.cos(ang_r), jnp.cos(ang_c), jnp.cos(ang_c)], axis=1)
    sin64 = jnp.concatenate([-jnp.sin(ang_r), jnp.sin(ang_r), -jnp.sin(ang_c), jnp.sin(ang_c)], axis=1)
    cos = jnp.concatenate([jnp.ones((ROW_BLOCK, HEAD_DIM), F32), cos64], axis=0)
    sin = jnp.concatenate([jnp.zeros((ROW_BLOCK, HEAD_DIM), F32), sin64], axis=0)
    return jnp.tile(cos, (1, 2)), jnp.tile(sin, (1, 2))


def _qk_prep(lay, p, width, gains, cos, sin):
    n_ctx_blocks = lay.nctx // ROW_BLOCK
    per_seq = lay.tl // ROW_BLOCK

    def tab(i):
        return (jnp.where(i < n_ctx_blocks, 0, 1 + (i - n_ctx_blocks) % per_seq), 0)

    seg = jnp.arange(LANE) // HEAD_DIM
    bd = (seg[:, None] == seg[None, :]).astype(F32) / HEAD_DIM
    return pl.pallas_call(
        functools.partial(_prep_body, n_chunks=width // LANE),
        grid=(lay.nt // ROW_BLOCK,),
        in_specs=[pl.BlockSpec((ROW_BLOCK, width), lambda i: (i, 0)),
                  pl.BlockSpec((1, width), lambda i: (0, 0)),
                  pl.BlockSpec((ROW_BLOCK, LANE), tab),
                  pl.BlockSpec((ROW_BLOCK, LANE), tab),
                  pl.BlockSpec((LANE, LANE), lambda i: (0, 0))],
        out_specs=pl.BlockSpec((ROW_BLOCK, width), lambda i: (i, 0)),
        out_shape=jax.ShapeDtypeStruct((lay.nt, width), F32),
        compiler_params=_params(1),
        name="qk_prep",
    )(p, gains, cos, sin, bd)


def _dup_kv(x):
    lo = _lo_lanes(x.shape)
    xr = pltpu.roll(x, HEAD_DIM, 1)
    return jnp.where(lo, x, xr), jnp.where(lo, xr, x)


def _gqa_heads(sink_ref, q_ref, k2, v2, bias, o_ref):
    rows = q_ref.shape[0]
    lo = _lo_lanes((rows, LANE))
    for pair in range(A_HEADS // 2):
        kv = pair // 2
        cols = slice(pair * LANE, (pair + 1) * LANE)
        qp = q_ref[:, cols] * ATTN_SCALE
        outs = []
        for half in range(2):
            qm = jnp.where(lo, qp, 0.0) if half == 0 else jnp.where(lo, 0.0, qp)
            s = _dot_nt(qm, k2[kv])
            if bias is not None:
                s = s + bias
            sk = sink_ref[2 * pair + half]
            mx = jnp.maximum(jnp.max(s, axis=-1, keepdims=True), sk)
            p = jnp.exp(s - mx)
            den = jnp.sum(p, axis=-1, keepdims=True) + jnp.exp(sk - mx)
            outs.append(_dot(p, v2[kv]) / den)
        o_ref[:, cols] = jnp.where(lo, outs[0], outs[1])


def _attn_a_ctx_body(sink_ref, q_ref, k_ref, v_ref, o_ref):
    _gqa_heads(sink_ref, q_ref, _dup_kv(k_ref[...]), _dup_kv(v_ref[...]), None, o_ref)


def _attn_a_ctx(lay, qk, p, sink):
    return pl.pallas_call(
        _attn_a_ctx_body,
        grid=(lay.bc,),
        in_specs=[pl.BlockSpec(memory_space=pltpu.SMEM),
                  pl.BlockSpec((lay.tc, 512), lambda b: (b, 0)),
                  pl.BlockSpec((lay.tc, LANE), lambda b: (b, E_AK // LANE)),
                  pl.BlockSpec((lay.tc, LANE), lambda b: (b, E_AV // LANE))],
        out_specs=pl.BlockSpec((lay.tc, 512), lambda b: (b, 0)),
        out_shape=jax.ShapeDtypeStruct((lay.nctx, 512), F32),
        compiler_params=_params(1),
        name="attn_a_ctx",
    )(sink, qk, qk, p)


def _attn_a_lat_body(sink_ref, q_ref, k0, k1, k2r, v0, v1, v2r, kc_ref, vc_ref, o_ref):
    n = pl.program_id(1)
    last = pl.num_programs(1) - 1
    w = ATTN_Q_BLOCK
    lc = kc_ref.shape[0]
    k_all = jnp.concatenate([k0[...], k1[...], k2r[...], kc_ref[...]], axis=0)
    v_all = jnp.concatenate([v0[...], v1[...], v2r[...], vc_ref[...]], axis=0)
    i = lax.broadcasted_iota(jnp.int32, (w, 3 * w + lc), 0)
    j = lax.broadcasted_iota(jnp.int32, (w, 3 * w + lc), 1)
    valid = (jnp.abs(i + w - j) <= A_WINDOW) & ((j >= w) | (n > 0)) & ((j < 2 * w) | (n < last))
    bias = jnp.where(valid | (j >= 3 * w), 0.0, NEG_BIG)
    _gqa_heads(sink_ref, q_ref, _dup_kv(k_all), _dup_kv(v_all), bias, o_ref)


def _attn_a_lat(lay, qk, p, sink, cache_k, cache_v, layer):
    w = ATTN_Q_BLOCK
    nb = lay.tl // w
    base = lay.nctx // w
    lc = cache_k.shape[2]

    def row(off):
        return lambda b, n: base + b * nb + jnp.clip(n + off, 0, nb - 1)

    def kspec(off):
        r = row(off)
        return pl.BlockSpec((w, LANE), lambda b, n: (r(b, n), E_AK // LANE))

    def vspec(off):
        r = row(off)
        return pl.BlockSpec((w, LANE), lambda b, n: (r(b, n), E_AV // LANE))

    cspec = pl.BlockSpec((None, None, lc, LANE), lambda b, n: (b, layer, 0, 0))
    r0 = row(0)
    return pl.pallas_call(
        _attn_a_lat_body,
        grid=(lay.bl, nb),
        in_specs=[pl.BlockSpec(memory_space=pltpu.SMEM),
                  pl.BlockSpec((w, 512), lambda b, n: (r0(b, n), 0)),
                  kspec(-1), kspec(0), kspec(1), vspec(-1), vspec(0), vspec(1), cspec, cspec],
        out_specs=pl.BlockSpec((w, 512), lambda b, n: (b * nb + n, 0)),
        out_shape=jax.ShapeDtypeStruct((lay.bl * lay.tl, 512), F32),
        compiler_params=_params(2),
        name="attn_a_lat",
    )(sink, qk, qk, qk, qk, p, p, p,
      cache_k.reshape(cache_k.shape[:3] + (LANE,)), cache_v.reshape(cache_v.shape[:3] + (LANE,)))


def _gla_body(*refs, reverse, n_ctx_blocks, per_seq):
    if reverse:
        (bq_ref, bk_ref, bv_ref, bg_ref, gw_ref, gb_ref, s0_ref, tri_ref, of_ref, br_ref, bon_ref,
         o_ref, sfin_ref, st) = refs
    else:
        bq_ref, bk_ref, bv_ref, bg_ref, gw_ref, gb_ref, s0_ref, tri_ref, o_ref, sfin_ref, st = refs
    i = pl.program_id(0)
    is_start = (i < n_ctx_blocks) | (((i - n_ctx_blocks) % per_seq) == 0)

    @pl.when(is_start)
    def _():
        st[...] = s0_ref[...]

    z = _dot_hi(bg_ref[...], gw_ref[...]) + gb_ref[...]
    lg = (jnp.minimum(z, 0.0) - jnp.log1p(jnp.exp(-jnp.abs(z)))) * (1.0 / B_GATE_TAU)
    q = bq_ref[...] * (B_DK ** -0.5)
    k = bk_ref[...]
    tri = tri_ref[...]
    L = B_CHUNK
    n_chunks = ROW_BLOCK // L
    lo = _lo_lanes((L, LANE))
    ri = lax.broadcasted_iota(jnp.int32, (L, L), 0)
    ci = lax.broadcasted_iota(jnp.int32, (L, L), 1)
    causal = (ri <= ci) if reverse else (ri >= ci)
    order = range(n_chunks - 1, -1, -1) if reverse else range(n_chunks)
    for c in order:
        rows = slice(c * L, (c + 1) * L)
        b = _dot_hi(tri, lg[rows])
        b_last = b[0:1] if reverse else b[L - 1:L]
        qd = q[rows] * jnp.exp(b)
        kd = k[rows] * jnp.exp(-b)
        kt = k[rows] * jnp.exp(b_last - b)
        dec = jnp.exp(b_last)
        for h in range(B_HEADS):
            cols = slice((h // 2) * LANE, (h // 2 + 1) * LANE)
            mine = lo if h % 2 == 0 else jnp.logical_not(lo)
            qm = jnp.where(mine, qd[:, cols], 0.0)
            a = jnp.where(causal, _dot_nt(qm, kd[:, cols]), 0.0)
            vh = bv_ref[rows, h * B_DV:(h + 1) * B_DV]
            s_t = st[h]
            o = _dot(a, vh) + _dot_nt(qm, s_t)
            kv_t = _dot(vh.T, jnp.where(mine, kt[:, cols], 0.0))
            st[h] = s_t * dec[:, cols] + kv_t
            if reverse:
                tot = of_ref[rows, h * B_DV:(h + 1) * B_DV] + o
                ms = jnp.mean(tot * tot, axis=-1, keepdims=True)
                brh = br_ref[rows, h * B_DV:(h + 1) * B_DV]
                o = tot * lax.rsqrt(ms + RMS_EPS) * bon_ref[...] * (brh * jax.nn.sigmoid(brh))
            o_ref[rows, h * B_DV:(h + 1) * B_DV] = o
    sfin_ref[...] = st[...]


def _gla(lay, p, gw_pad, gb, s0, reverse, o_fwd=None, b_on=None):
    n_ctx_blocks = lay.nctx // ROW_BLOCK
    per_seq = lay.tl // ROW_BLOCK
    n_blocks = lay.nt // ROW_BLOCK

    def blk(i):
        j = i - n_ctx_blocks
        within = j % per_seq
        lat = n_ctx_blocks + (j - within) + (per_seq - 1 - within if reverse else within)
        return jnp.where(i < n_ctx_blocks, i, lat)

    def seq(i):
        return jnp.where(i < n_ctx_blocks, i, n_ctx_blocks + (i - n_ctx_blocks) // per_seq)

    def s0_idx(i):
        return jnp.where(i < n_ctx_blocks, lay.bl, (i - n_ctx_blocks) // per_seq)

    ri = jnp.arange(B_CHUNK)
    tri = ((ri[:, None] <= ri[None, :]) if reverse else (ri[:, None] >= ri[None, :])).astype(F32)
    in_specs = [pl.BlockSpec((ROW_BLOCK, 256), lambda i: (blk(i), E_BQ // 256)),
                pl.BlockSpec((ROW_BLOCK, 256), lambda i: (blk(i), E_BK // 256)),
                pl.BlockSpec((ROW_BLOCK, 512), lambda i: (blk(i), E_BV // 512)),
                pl.BlockSpec((ROW_BLOCK, LANE), lambda i: (blk(i), E_BG // LANE)),
                pl.BlockSpec((LANE, 256), lambda i: (0, 0)),
                pl.BlockSpec((1, 256), lambda i: (0, 0)),
                pl.BlockSpec((None, B_HEADS, LANE, LANE), lambda i: (s0_idx(i), 0, 0, 0)),
                pl.BlockSpec((B_CHUNK, B_CHUNK), lambda i: (0, 0))]
    args = [p, p, p, p, gw_pad, gb.reshape(1, 256), s0, tri]
    if reverse:
        in_specs += [pl.BlockSpec((ROW_BLOCK, 512), lambda i: (blk(i), 0)),
                     pl.BlockSpec((ROW_BLOCK, 512), lambda i: (blk(i), E_BR // 512)),
                     pl.BlockSpec((1, B_DV), lambda i: (0, 0))]
        args += [o_fwd, p, b_on.reshape(1, B_DV)]
    n_seq = n_ctx_blocks + lay.bl
    return pl.pallas_call(
        functools.partial(_gla_body, reverse=reverse, n_ctx_blocks=n_ctx_blocks, per_seq=per_seq),
        grid=(n_blocks,),
        in_specs=in_specs,
        out_specs=[pl.BlockSpec((ROW_BLOCK, 512), lambda i: (blk(i), 0)),
                   pl.BlockSpec((None, B_HEADS, LANE, LANE), lambda i: (seq(i), 0, 0, 0))],
        out_shape=[jax.ShapeDtypeStruct((lay.nt, 512), F32),
                   jax.ShapeDtypeStruct((n_seq, B_HEADS, LANE, LANE), F32)],
        scratch_shapes=[pltpu.VMEM((B_HEADS, LANE, LANE), F32)],
        compiler_params=_params(1),
        name="gla_bwd" if reverse else "gla_fwd",
    )(*args)


def _state_in(s):
    st = jnp.swapaxes(s.astype(F32), 2, 3)
    z = jnp.zeros_like(st)
    even = jnp.concatenate([st, z], axis=-1)
    odd = jnp.concatenate([z, st], axis=-1)
    pick = (jnp.arange(B_HEADS) % 2 == 0)[None, :, None, None]
    full = jnp.where(pick, even, odd)
    return jnp.concatenate([full, jnp.zeros_like(full[:1])], axis=0)


def _state_out(s_t):
    even = s_t[..., :B_DK]
    odd = s_t[..., B_DK:]
    pick = (jnp.arange(B_HEADS) % 2 == 0)[None, :, None, None]
    return jnp.swapaxes(jnp.where(pick, even, odd), 2, 3)


def _linear_res_body(*refs, n_in):
    a_refs, w_refs = refs[:n_in], refs[n_in:2 * n_in]
    gate_ref, res_ref, o_ref = refs[2 * n_in:]
    acc = _dot(a_refs[0][...], w_refs[0][...])
    for a_ref, w_ref in zip(a_refs[1:], w_refs[1:]):
        acc = acc + _dot(a_ref[...], w_ref[...])
    o_ref[...] = res_ref[...] + gate_ref[0] * acc


def _linear_res(lay, xs, ws, mod, chunk, res):
    n_in = len(xs)
    in_specs = [pl.BlockSpec((LIN_BLOCK, a.shape[1]), lambda i: (i, 0)) for a in xs]
    in_specs += [pl.BlockSpec(w.shape, lambda i: (0, 0)) for w in ws]
    in_specs += [_mod_spec(lay, LIN_BLOCK, chunk),
                 pl.BlockSpec((LIN_BLOCK, D_MODEL), lambda i: (i, 0))]
    return pl.pallas_call(
        functools.partial(_linear_res_body, n_in=n_in),
        grid=(lay.nt // LIN_BLOCK,),
        in_specs=in_specs,
        out_specs=pl.BlockSpec((LIN_BLOCK, D_MODEL), lambda i: (i, 0)),
        out_shape=jax.ShapeDtypeStruct((lay.nt, D_MODEL), F32),
        compiler_params=_params(1),
        name="linear_res",
    )(*xs, *ws, mod, res)


def _diff_attn_body(*refs, lam_init, has_cache):
    if has_cache:
        lamv_ref, con_ref, q_ref, k_ref, v_ref, kc_ref, vc_ref, o_ref = refs
    else:
        lamv_ref, con_ref, q_ref, k_ref, v_ref, o_ref = refs
    lv = lamv_ref[...]
    lam = (jnp.exp(jnp.sum(lv[0:1] * lv[1:2], axis=-1, keepdims=True))
           - jnp.exp(jnp.sum(lv[2:3] * lv[3:4], axis=-1, keepdims=True)) + lam_init)
    q = q_ref[...] * ATTN_SCALE
    tq = q.shape[0]
    lo = _lo_lanes(q.shape)
    qs = jnp.concatenate([jnp.where(lo, q, 0.0), jnp.where(lo, 0.0, q)], axis=0)
    s = _dot_nt(qs, k_ref[...])
    mx = jnp.max(s, axis=-1, keepdims=True)
    if has_cache:
        sc = _dot_nt(qs, kc_ref[...])
        mx = jnp.maximum(mx, jnp.max(sc, axis=-1, keepdims=True))
    p = jnp.exp(s - mx)
    den = jnp.sum(p, axis=-1, keepdims=True)
    if has_cache:
        pc = jnp.exp(sc - mx)
        den = den + jnp.sum(pc, axis=-1, keepdims=True)
    inv = 1.0 / den
    w0 = inv[:tq]
    w1 = lam * inv[tq:]
    o = _dot(p[:tq] * w0 - p[tq:] * w1, v_ref[...])
    if has_cache:
        o = o + _dot(pc[:tq] * w0 - pc[tq:] * w1, vc_ref[...])
    ms = jnp.mean(o * o, axis=-1, keepdims=True)
    o_ref[...] = o * lax.rsqrt(ms + RMS_EPS) * con_ref[...] * (1.0 - lam_init)


def _diff_attn(qk, p, lamv, c_on, lam_init, n_seq, t_seq, row_base, tq, cache=None):
    nq = t_seq // tq
    qbase = row_base // tq
    kbase = row_base // t_seq
    in_specs = [pl.BlockSpec((8, LANE), lambda b, h, i: (0, 0)),
                pl.BlockSpec((1, LANE), lambda b, h, i: (0, 0)),
                pl.BlockSpec((tq, LANE), lambda b, h, i: (qbase + b * nq + i, h)),
                pl.BlockSpec((t_seq, LANE), lambda b, h, i: (kbase + b, C_HEADS + h)),
                pl.BlockSpec((t_seq, LANE), lambda b, h, i: (kbase + b, 2 * C_HEADS + h))]
    args = [lamv, c_on.reshape(1, C_DV), qk, qk, p]
    if cache is not None:
        ck, cv, layer = cache
        lc = ck.shape[2]
        in_specs += [pl.BlockSpec((None, None, lc, LANE), lambda b, h, i: (b, layer, 0, h)),
                     pl.BlockSpec((None, None, lc, LANE), lambda b, h, i: (b, layer, 0, h))]
        args += [ck.reshape(ck.shape[:3] + (C_HEADS * LANE,)), cv.reshape(cv.shape[:3] + (C_HEADS * LANE,))]
    return pl.pallas_call(
        functools.partial(_diff_attn_body, lam_init=lam_init, has_cache=cache is not None),
        grid=(n_seq, C_HEADS, nq),
        in_specs=in_specs,
        out_specs=pl.BlockSpec((tq, LANE), lambda b, h, i: (b * nq + i, h)),
        out_shape=jax.ShapeDtypeStruct((n_seq * t_seq, C_HEADS * C_DV), F32),
        compiler_params=_params(3),
        name="diff_attn",
    )(*args)


def _top_rows(s, k, out_ref):
    cur = s
    first = None
    for r in range(k):
        m = jnp.max(cur, axis=0, keepdims=True)
        out_ref[r:r + 1, :] = m
        first = m if first is None else first
        cur = jnp.where(cur == m, NEG_BIG, cur)
    return first, m


def _peer_body(x_ref, g_ref, sc_ref, sh_ref, gate_ref, wq_ref, sk_ref, u_ref, vt_ref, o_ref,
               h_s, s0_s, e0_s, s1_s, e1_s, thr_s, top0_s, top1_s, cand_s, st_s, wt_s, acc_s):
    eb = pl.program_id(1)
    n_eb = pl.num_programs(1)
    tn = x_ref.shape[0]

    @pl.when(eb == 0)
    def _():
        h = _rms_mod(x_ref[...], g_ref[...], sc_ref[0], sh_ref[0])
        h_s[...] = h.astype(BF16)
        q = _dot(h, wq_ref[...])
        for hd in range(P_HEADS):
            qp = q[:, hd * LANE:(hd + 1) * LANE]
            s_t = lax.dot_general(sk_ref[hd], qp, (((1,), (1,)), ((), ())), precision=HI,
                                  preferred_element_type=F32)
            s0 = s_t[:P_NKEYS]
            s1 = s_t[P_NKEYS:]
            max0, _ = _top_rows(s0, P_TOPK, top0_s)
            max1, _ = _top_rows(s1, P_TOPK, top1_s)
            top1 = top1_s[...]
            for a in range(P_TOPK):
                cand_s[a * P_TOPK:(a + 1) * P_TOPK, :] = top0_s[a:a + 1, :] + top1
            cand = cand_s[...]
            _, thr = _top_rows(cand, P_TOPK, top0_s)
            mx = max0 + max1
            z = jnp.sum(jnp.where(cand >= thr, jnp.exp(cand - mx), 0.0), axis=0, keepdims=True)
            s0_s[hd] = s0
            s1_s[hd] = s1
            e0_s[hd] = jnp.exp(s0 - max0)
            e1_s[hd] = jnp.exp(s1 - max1) / z
            thr_s[hd:hd + 1, :] = thr
        acc_s[...] = jnp.zeros_like(acc_s)

    st_s[...] = _dot_nt(u_ref[...], h_s[...])
    n_i = u_ref.shape[0] // P_NKEYS
    assert n_i == SUBLANES
    group = pl.ds(pl.multiple_of(eb * SUBLANES, SUBLANES), SUBLANES)
    for il in range(n_i):
        rows = slice(il * P_NKEYS, (il + 1) * P_NKEYS)

        def tile(t, carry):
            cols = pl.ds(pl.multiple_of(t * LANE, LANE), LANE)
            g = jnp.zeros((P_NKEYS, LANE), F32)
            for hd in range(P_HEADS):
                s0r = s0_s[hd, group, cols][il:il + 1]
                e0r = e0_s[hd, group, cols][il:il + 1]
                keep = (s0r + s1_s[hd, :, cols]) >= thr_s[hd:hd + 1, cols]
                g = g + e0r * jnp.where(keep, e1_s[hd, :, cols], 0.0)
            sc = st_s[rows, cols]
            act = 0.5 * sc * (1.0 + lax.erf(sc * (2.0 ** -0.5)))
            wt_s[rows, cols] = (g * act).astype(BF16)
            return carry

        lax.fori_loop(0, tn // LANE, tile, 0)
    acc_s[...] += _dot(vt_ref[...], wt_s[...])

    @pl.when(eb == n_eb - 1)
    def _():
        o_ref[...] = x_ref[...] + gate_ref[0] * acc_s[...].T


def _peer(lay, x, g, mod, wq, sk2, u, vt):
    tn, en = PEER_TOKENS, PEER_EXPERTS
    n_exp = u.shape[0]
    grid = (lay.nt // tn, n_exp // en)
    head_scr = pltpu.VMEM((P_HEADS, P_NKEYS, tn), F32)
    return pl.pallas_call(
        _peer_body,
        grid=grid,
        in_specs=[pl.BlockSpec((tn, D_MODEL), lambda i, e: (i, 0)),
                  pl.BlockSpec((1, D_MODEL), lambda i, e: (0, 0)),
                  _mod_spec(lay, tn, 4), _mod_spec(lay, tn, 3), _mod_spec(lay, tn, 5),
                  pl.BlockSpec((D_MODEL, P_HEADS * LANE), lambda i, e: (0, 0)),
                  pl.BlockSpec((P_HEADS, 2 * P_NKEYS, LANE), lambda i, e: (0, 0, 0)),
                  pl.BlockSpec((en, D_MODEL), lambda i, e: (e, 0)),
                  pl.BlockSpec((D_MODEL, en), lambda i, e: (0, e))],
        out_specs=pl.BlockSpec((tn, D_MODEL), lambda i, e: (i, 0)),
        out_shape=jax.ShapeDtypeStruct((lay.nt, D_MODEL), F32),
        scratch_shapes=[pltpu.VMEM((tn, D_MODEL), BF16),
                        head_scr, head_scr, head_scr, head_scr,
                        pltpu.VMEM((P_HEADS, tn), F32),
                        pltpu.VMEM((P_TOPK, tn), F32),
                        pltpu.VMEM((P_TOPK, tn), F32),
                        pltpu.VMEM((P_TOPK * P_TOPK, tn), F32),
                        pltpu.VMEM((en, tn), F32),
                        pltpu.VMEM((en, tn), BF16),
                        pltpu.VMEM((D_MODEL, tn), F32)],
        compiler_params=_params(2),
        name="peer",
    )(x, g.reshape(1, D_MODEL), mod, mod, mod, wq, sk2, u, vt)


def _sub_key_blocks(sub_keys):
    z = jnp.zeros_like(sub_keys[:, 0])
    top = jnp.concatenate([sub_keys[:, 0], z], axis=-1)
    bot = jnp.concatenate([z, sub_keys[:, 1]], axis=-1)
    return jnp.concatenate([top, bot], axis=1)


def _lambda_init(layer):
    return 0.8 - 0.6 * math.exp(-0.3 * layer)


def _even_w_in(w):
    aq, ak, av, bq, bk, bv, br, bg = jnp.split(w, [512, 640, 768, 1024, 1280, 1792, 2304], axis=1)
    pad = jnp.zeros((w.shape[0], E_WIDTH - E_BG - bg.shape[1]), w.dtype)
    return jnp.concatenate([aq, ak, av, bq, bv, br, bk, bg, pad], axis=1).astype(BF16)


def kernel(x_prompt, x_sample, cache_a_k, cache_a_v, state_b_fwd, state_b_bwd, cache_c_k, cache_c_v, c, c_ctx, ada_w, ada_b, norm_mix_g, norm_ffn_g, e_w_in, e_w_out, a_q_norm, a_k_norm, a_sink, b_gate_w_f, b_gate_b_f, b_gate_w_b, b_gate_b_b, b_out_norm, o_w_in, o_w_out, c_q_norm, c_k_norm, c_lambda_q1, c_lambda_k1, c_lambda_q2, c_lambda_k2, c_out_norm, p_w_q, p_sub_keys, p_u, p_v):
    bc, tc, d = x_prompt.shape
    bl, tl, _ = x_sample.shape
    depth = ada_w.shape[0]
    lay = _Layout(bc, tc, bl, tl)
    n_mod_rows = 8 * ((bl + 1 + 7) // 8)
    cvecs = jnp.concatenate([c, c_ctx[None, :], jnp.zeros((n_mod_rows - bl - 1, d), F32)], axis=0)
    mods = _modulation(cvecs, ada_w, ada_b)
    cos, sin = _rope_tables(tl)
    x = jnp.concatenate([x_prompt.reshape(lay.nctx, d), x_sample.reshape(bl * tl, d)], axis=0)

    ak_l, av_l, sf_l, sb_l, ck_l, cv_l = [], [], [], [], [], []
    for l in range(depth):
        i = l // 2
        mod = mods[l].reshape(n_mod_rows, 1, N_MOD * d)
        if l % 2 == 0:
            p = _ln_proj(lay, x, norm_mix_g[l], mod, 0, _even_w_in(e_w_in[i]))
            gains = jnp.concatenate([jnp.tile(a_q_norm[i], A_HEADS), jnp.tile(a_k_norm[i], A_KV_HEADS)])
            qk = _qk_prep(lay, p, E_QK_WIDTH, gains.reshape(1, E_QK_WIDTH), cos, sin)
            sink = a_sink[i].reshape(A_HEADS)
            oa = jnp.concatenate([_attn_a_ctx(lay, qk, p, sink),
                                  _attn_a_lat(lay, qk, p, sink, cache_a_k, cache_a_v, i)], axis=0)
            zrow = jnp.zeros((LANE - 2 * B_GATE_RANK, 256), F32)
            zgate = jnp.zeros((B_GATE_RANK, 256), F32)
            gw_f = jnp.concatenate([b_gate_w_f[i], zgate, zrow], axis=0)
            gw_b = jnp.concatenate([zgate, b_gate_w_b[i], zrow], axis=0)
            o_f, s_f = _gla(lay, p, gw_f, b_gate_b_f[i], _state_in(state_b_fwd[:, i]), False)
            ob, s_b = _gla(lay, p, gw_b, b_gate_b_b[i], _state_in(state_b_bwd[:, i]), True,
                           o_fwd=o_f, b_on=b_out_norm[i])
            w_out = e_w_out[i].astype(BF16)
            x = _linear_res(lay, [oa, ob], [w_out[:512], w_out[512:]], mod, 2, x)
            ak_l.append(qk[:lay.nctx, E_AK:E_AK + LANE].reshape(bc, tc, A_KV_HEADS, HEAD_DIM))
            av_l.append(p[:lay.nctx, E_AV:E_AV + LANE].reshape(bc, tc, A_KV_HEADS, HEAD_DIM))
            sf_l.append(_state_out(s_f[:bc]))
            sb_l.append(_state_out(s_b[:bc]))
        else:
            lam_init = _lambda_init(l)
            p = _ln_proj(lay, x, norm_mix_g[l], mod, 0, o_w_in[i].astype(BF16))
            gains = jnp.concatenate([jnp.tile(c_q_norm[i], 2 * C_HEADS), jnp.tile(c_k_norm[i], 2 * C_HEADS)])
            qk = _qk_prep(lay, p, O_QK_WIDTH, gains.reshape(1, O_QK_WIDTH), cos, sin)
            lamv = jnp.stack([c_lambda_q1[i], c_lambda_k1[i], c_lambda_q2[i], c_lambda_k2[i]])
            lamv = jnp.pad(lamv, ((0, 4), (0, LANE - HEAD_DIM)))
            o_ctx = _diff_attn(qk, p, lamv, c_out_norm[i], lam_init, bc, tc, 0, tc)
            o_lat = _diff_attn(qk, p, lamv, c_out_norm[i], lam_init, bl, tl, lay.nctx, ATTN_Q_BLOCK,
                               cache=(cache_c_k, cache_c_v, i))
            o = jnp.concatenate([o_ctx, o_lat], axis=0)
            x = _linear_res(lay, [o], [o_w_out[i].astype(BF16)], mod, 2, x)
            ck_l.append(qk[:lay.nctx, 1024:2048].reshape(bc, tc, C_HEADS, 2, HEAD_DIM))
            cv_l.append(p[:lay.nctx, 2048:3072].reshape(bc, tc, C_HEADS, C_DV))
        x = _peer(lay, x, norm_ffn_g[l], mod, p_w_q[l].astype(BF16), _sub_key_blocks(p_sub_keys[l]),
                  p_u[l].astype(BF16), p_v[l].T.astype(BF16))

    return (x[:lay.nctx].reshape(bc, tc, d), x[lay.nctx:].reshape(bl, tl, d),
            jnp.stack(ak_l, axis=1), jnp.stack(av_l, axis=1), jnp.stack(sf_l, axis=1), jnp.stack(sb_l, axis=1),
            jnp.stack(ck_l, axis=1), jnp.stack(cv_l, axis=1))
```

```python
import functools
import math

import jax
import jax.numpy as jnp
from jax import lax
from jax.experimental import pallas as pl
from jax.experimental.pallas import tpu as pltpu

F32 = jnp.float32
BF16 = jnp.bfloat16
HI = lax.Precision.HIGHEST

D_MODEL = 1024
GRID_W = 64
HEAD_DIM = 64
ATTN_SCALE = HEAD_DIM ** -0.5
ROPE_BASE = 10000.0
RMS_EPS = 1e-6
N_MOD = 6
A_HEADS = 8
A_KV_HEADS = 2
A_WINDOW = 128
B_HEADS = 4
B_DK = 64
B_DV = 128
B_GATE_RANK = 16
B_GATE_TAU = 16.0
B_CHUNK = 64
C_HEADS = 8
C_DV = 128
P_HEADS = 8
P_NKEYS = 128
P_TOPK = 16

LANE = 128
SUBLANES = 8
ROW_BLOCK = 256
LIN_BLOCK = 512
PEER_TOKENS = 512
PEER_EXPERTS = SUBLANES * P_NKEYS
GATE_ROWS = 16
ATTN_Q_BLOCK = 128
NEG_BIG = -1e30
VMEM_LIMIT = 56 * 1024 * 1024

E_AQ, E_AK, E_AV, E_BQ, E_BV, E_BR, E_BK, E_BG, E_WIDTH = 0, 512, 640, 768, 1024, 1536, 2048, 2304, 2432
E_QK_WIDTH = 640
O_QK_WIDTH = 2048


def _dot(a, b):
    return lax.dot_general(a.astype(BF16), b.astype(BF16), (((1,), (0,)), ((), ())),
                           preferred_element_type=F32)


def _dot_nt(a, b):
    return lax.dot_general(a.astype(BF16), b.astype(BF16), (((1,), (1,)), ((), ())),
                           preferred_element_type=F32)


def _dot_hi(a, b):
    return lax.dot_general(a, b, (((1,), (0,)), ((), ())), precision=HI, preferred_element_type=F32)


def _params(n_axes, flags=None):
    return pltpu.CompilerParams(dimension_semantics=("arbitrary",) * n_axes,
                                vmem_limit_bytes=VMEM_LIMIT, flags=flags)


def _lo_lanes(shape):
    return (lax.broadcasted_iota(jnp.int32, shape, len(shape) - 1) & 64) == 0


def _mod_body(c_ref, w_ref, b_ref, o_ref):
    c = c_ref[...]
    s = c * jax.nn.sigmoid(c)
    o_ref[0] = _dot_hi(s, w_ref[0]) + b_ref[0]


def _modulation(cvecs, ada_w, ada_b):
    depth, d, n = ada_w.shape
    r = cvecs.shape[0]
    tn = 1536
    return pl.pallas_call(
        _mod_body,
        grid=(depth, n // tn),
        in_specs=[pl.BlockSpec((r, d), lambda l, j: (0, 0)),
                  pl.BlockSpec((1, d, tn), lambda l, j: (l, 0, j)),
                  pl.BlockSpec((1, 1, tn), lambda l, j: (l, 0, j))],
        out_specs=pl.BlockSpec((1, r, tn), lambda l, j: (l, 0, j)),
        out_shape=jax.ShapeDtypeStruct((depth, r, n), F32),
        compiler_params=_params(2),
        name="modulation",
    )(cvecs, ada_w, ada_b.reshape(depth, 1, n))


class _Layout:
    def __init__(self, bc, tc, bl, tl):
        self.bc, self.tc, self.bl, self.tl = bc, tc, bl, tl
        self.nctx = bc * tc
        self.nt = self.nctx + bl * tl
        assert tc == ROW_BLOCK and tl % LIN_BLOCK == 0 and self.nctx % LIN_BLOCK == 0
        assert self.nctx % tl == 0 and tl % GRID_W == 0 and self.nt % PEER_TOKENS == 0
        assert self.nctx % PEER_TOKENS == 0 and tl % PEER_TOKENS == 0

    def mod_row(self, i, block):
        nctx_blocks = self.nctx // block
        return jnp.where(i < nctx_blocks, self.bl, (i - nctx_blocks) // (self.tl // block))


def _mod_spec(lay, block, chunk):
    return pl.BlockSpec((1, 1, D_MODEL), lambda i, *_: (lay.mod_row(i, block), 0, chunk))


def _rms_mod(x, g, sc, sh):
    ms = jnp.mean(x * x, axis=-1, keepdims=True)
    return (x * lax.rsqrt(ms + RMS_EPS) * g) * (1.0 + sc) + sh


def _ln_proj_body(x_ref, g_ref, sc_ref, sh_ref, w_ref, o_ref):
    h = _rms_mod(x_ref[...], g_ref[...], sc_ref[0], sh_ref[0])
    o_ref[...] = _dot(h, w_ref[...])


def _ln_proj(lay, x, g, mod, chunk, w):
    n_out = w.shape[1]
    return pl.pallas_call(
        _ln_proj_body,
        grid=(lay.nt // LIN_BLOCK,),
        in_specs=[pl.BlockSpec((LIN_BLOCK, D_MODEL), lambda i: (i, 0)),
                  pl.BlockSpec((1, D_MODEL), lambda i: (0, 0)),
                  _mod_spec(lay, LIN_BLOCK, chunk + 1),
                  _mod_spec(lay, LIN_BLOCK, chunk),
                  pl.BlockSpec((D_MODEL, n_out), lambda i: (0, 0))],
        out_specs=pl.BlockSpec((LIN_BLOCK, n_out), lambda i: (i, 0)),
        out_shape=jax.ShapeDtypeStruct((lay.nt, n_out), F32),
        compiler_params=_params(1),
        name="ln_proj",
    )(x, g.reshape(1, D_MODEL), mod, mod, w)


def _prep_body(x_ref, g_ref, cos_ref, sin_ref, bd_ref, o_ref, *, n_chunks):
    c = cos_ref[...]
    s = sin_ref[...]
    first = (lax.broadcasted_iota(jnp.int32, c.shape, 1) & 16) == 0
    for j in range(n_chunks):
        cols = slice(j * LANE, (j + 1) * LANE)
        x = x_ref[:, cols]
        ms = _dot_hi(x * x, bd_ref[...])
        y = x * lax.rsqrt(ms + RMS_EPS) * g_ref[:, cols]
        partner = jnp.where(first, pltpu.roll(y, LANE - 16, 1), pltpu.roll(y, 16, 1))
        o_ref[:, cols] = y * c + partner * s


def _rope_tables(tl):
    t = jnp.arange(tl)
    row = (t // GRID_W).astype(F32)
    col = (t % GRID_W).astype(F32)
    nf = HEAD_DIM // 4
    freqs = ROPE_BASE ** (-jnp.arange(nf, dtype=F32) / nf)
    ang_r = row[:, None] * freqs
    ang_c = col[:, None] * freqs
    cos64 = jnp.concatenate([jnp.cos(ang_r), jnp.cos(ang_r), jnp.cos(ang_c), jnp.cos(ang_c)], axis=1)
    sin64 = jnp.concatenate([-jnp.sin(ang_r), jnp.sin(ang_r), -jnp.sin(ang_c), jnp.sin(ang_c)], axis=1)
    cos = jnp.concatenate([jnp.ones((ROW_BLOCK, HEAD_DIM), F32), cos64], axis=0)
    sin = jnp.concatenate([jnp.zeros((ROW_BLOCK, HEAD_DIM), F32), sin64], axis=0)
    return jnp.tile(cos, (1, 2)), jnp.tile(sin, (1, 2))


def _qk_prep(lay, p, width, gains, cos, sin):
    n_ctx_blocks = lay.nctx // ROW_BLOCK
    per_seq = lay.tl // ROW_BLOCK

    def tab(i):
        return (jnp.where(i < n_ctx_blocks, 0, 1 + (i - n_ctx_blocks) % per_seq), 0)

    seg = jnp.arange(LANE) // HEAD_DIM
    bd = (seg[:, None] == seg[None, :]).astype(F32) / HEAD_DIM
    return pl.pallas_call(
        functools.partial(_prep_body, n_chunks=width // LANE),
        grid=(lay.nt // ROW_BLOCK,),
        in_specs=[pl.BlockSpec((ROW_BLOCK, width), lambda i: (i, 0)),
                  pl.BlockSpec((1, width), lambda i: (0, 0)),
                  pl.BlockSpec((ROW_BLOCK, LANE), tab),
                  pl.BlockSpec((ROW_BLOCK, LANE), tab),
                  pl.BlockSpec((LANE, LANE), lambda i: (0, 0))],
        out_specs=pl.BlockSpec((ROW_BLOCK, width), lambda i: (i, 0)),
        out_shape=jax.ShapeDtypeStruct((lay.nt, width), F32),
        compiler_params=_params(1),
        name="qk_prep",
    )(p, gains, cos, sin, bd)


def _dup_kv(x):
    lo = _lo_lanes(x.shape)
    xr = pltpu.roll(x, HEAD_DIM, 1)
    return jnp.where(lo, x, xr), jnp.where(lo, xr, x)


def _gqa_heads(sink_ref, q_ref, k2, v2, bias, o_ref):
    rows = q_ref.shape[0]
    lo = _lo_lanes((rows, LANE))
    for pair in range(A_HEADS // 2):
        kv = pair // 2
        cols = slice(pair * LANE, (pair + 1) * LANE)
        qp = q_ref[:, cols] * ATTN_SCALE
        outs = []
        for half in range(2):
            qm = jnp.where(lo, qp, 0.0) if half == 0 else jnp.where(lo, 0.0, qp)
            s = _dot_nt(qm, k2[kv])
            if bias is not None:
                s = s + bias
            sk = sink_ref[2 * pair + half]
            mx = jnp.maximum(jnp.max(s, axis=-1, keepdims=True), sk)
            p = jnp.exp(s - mx)
            den = jnp.sum(p, axis=-1, keepdims=True) + jnp.exp(sk - mx)
            outs.append(_dot(p, v2[kv]) / den)
        o_ref[:, cols] = jnp.where(lo, outs[0], outs[1])


def _attn_a_ctx_body(sink_ref, q_ref, k_ref, v_ref, o_ref):
    _gqa_heads(sink_ref, q_ref, _dup_kv(k_ref[...]), _dup_kv(v_ref[...]), None, o_ref)


def _attn_a_ctx(lay, qk, p, sink):
    return pl.pallas_call(
        _attn_a_ctx_body,
        grid=(lay.bc,),
        in_specs=[pl.BlockSpec(memory_space=pltpu.SMEM),
                  pl.BlockSpec((lay.tc, 512), lambda b: (b, 0)),
                  pl.BlockSpec((lay.tc, LANE), lambda b: (b, E_AK // LANE)),
                  pl.BlockSpec((lay.tc, LANE), lambda b: (b, E_AV // LANE))],
        out_specs=pl.BlockSpec((lay.tc, 512), lambda b: (b, 0)),
        out_shape=jax.ShapeDtypeStruct((lay.nctx, 512), F32),
        compiler_params=_params(1),
        name="attn_a_ctx",
    )(sink, qk, qk, p)


def _attn_a_lat_body(sink_ref, q_ref, k0, k1, k2r, v0, v1, v2r, kc_ref, vc_ref, o_ref):
    n = pl.program_id(1)
    last = pl.num_programs(1) - 1
    w = ATTN_Q_BLOCK
    lc = kc_ref.shape[0]
    k_all = jnp.concatenate([k0[...], k1[...], k2r[...], kc_ref[...]], axis=0)
    v_all = jnp.concatenate([v0[...], v1[...], v2r[...], vc_ref[...]], axis=0)
    i = lax.broadcasted_iota(jnp.int32, (w, 3 * w + lc), 0)
    j = lax.broadcasted_iota(jnp.int32, (w, 3 * w + lc), 1)
    valid = (jnp.abs(i + w - j) <= A_WINDOW) & ((j >= w) | (n > 0)) & ((j < 2 * w) | (n < last))
    bias = jnp.where(valid | (j >= 3 * w), 0.0, NEG_BIG)
    _gqa_heads(sink_ref, q_ref, _dup_kv(k_all), _dup_kv(v_all), bias, o_ref)


def _attn_a_lat(lay, qk, p, sink, cache_k, cache_v, layer):
    w = ATTN_Q_BLOCK
    nb = lay.tl // w
    base = lay.nctx // w
    lc = cache_k.shape[2]

    def row(off):
        return lambda b, n: base + b * nb + jnp.clip(n + off, 0, nb - 1)

    def kspec(off):
        r = row(off)
        return pl.BlockSpec((w, LANE), lambda b, n: (r(b, n), E_AK // LANE))

    def vspec(off):
        r = row(off)
        return pl.BlockSpec((w, LANE), lambda b, n: (r(b, n), E_AV // LANE))

    cspec = pl.BlockSpec((None, None, lc, LANE), lambda b, n: (b, layer, 0, 0))
    r0 = row(0)
    return pl.pallas_call(
        _attn_a_lat_body,
        grid=(lay.bl, nb),
        in_specs=[pl.BlockSpec(memory_space=pltpu.SMEM),
                  pl.BlockSpec((w, 512), lambda b, n: (r0(b, n), 0)),
                  kspec(-1), kspec(0), kspec(1), vspec(-1), vspec(0), vspec(1), cspec, cspec],
        out_specs=pl.BlockSpec((w, 512), lambda b, n: (b * nb + n, 0)),
        out_shape=jax.ShapeDtypeStruct((lay.bl * lay.tl, 512), F32),
        compiler_params=_params(2),
        name="attn_a_lat",
    )(sink, qk, qk, qk, qk, p, p, p,
      cache_k.reshape(cache_k.shape[:3] + (LANE,)), cache_v.reshape(cache_v.shape[:3] + (LANE,)))


def _gla_body(*refs, reverse, n_ctx_blocks, per_seq):
    if reverse:
        (bq_ref, bk_ref, bv_ref, bg_ref, gw_ref, gb_ref, s0_ref, tri_ref, of_ref, br_ref, bon_ref,
         o_ref, sfin_ref, st) = refs
    else:
        bq_ref, bk_ref, bv_ref, bg_ref, gw_ref, gb_ref, s0_ref, tri_ref, o_ref, sfin_ref, st = refs
    i = pl.program_id(0)
    is_start = (i < n_ctx_blocks) | (((i - n_ctx_blocks) % per_seq) == 0)

    @pl.when(is_start)
    def _():
        st[...] = s0_ref[...]

    z = _dot_hi(bg_ref[...], gw_ref[...]) + gb_ref[...]
    lg = (jnp.minimum(z, 0.0) - jnp.log1p(jnp.exp(-jnp.abs(z)))) * (1.0 / B_GATE_TAU)
    q = bq_ref[...] * (B_DK ** -0.5)
    k = bk_ref[...]
    tri = tri_ref[...]
    L = B_CHUNK
    n_chunks = ROW_BLOCK // L
    lo = _lo_lanes((L, LANE))
    ri = lax.broadcasted_iota(jnp.int32, (L, L), 0)
    ci = lax.broadcasted_iota(jnp.int32, (L, L), 1)
    causal = (ri <= ci) if reverse else (ri >= ci)
    order = range(n_chunks - 1, -1, -1) if reverse else range(n_chunks)
    for c in order:
        rows = slice(c * L, (c + 1) * L)
        b = _dot_hi(tri, lg[rows])
        b_last = b[0:1] if reverse else b[L - 1:L]
        qd = q[rows] * jnp.exp(b)
        kd = k[rows] * jnp.exp(-b)
        kt = k[rows] * jnp.exp(b_last - b)
        dec = jnp.exp(b_last)
        for h in range(B_HEADS):
            cols = slice((h // 2) * LANE, (h // 2 + 1) * LANE)
            mine = lo if h % 2 == 0 else jnp.logical_not(lo)
            qm = jnp.where(mine, qd[:, cols], 0.0)
            a = jnp.where(causal, _dot_nt(qm, kd[:, cols]), 0.0)
            vh = bv_ref[rows, h * B_DV:(h + 1) * B_DV]
            s_t = st[h]
            o = _dot(a, vh) + _dot_nt(qm, s_t)
            kv_t = _dot(vh.T, jnp.where(mine, kt[:, cols], 0.0))
            st[h] = s_t * dec[:, cols] + kv_t
            if reverse:
                tot = of_ref[rows, h * B_DV:(h + 1) * B_DV] + o
                ms = jnp.mean(tot * tot, axis=-1, keepdims=True)
                brh = br_ref[rows, h * B_DV:(h + 1) * B_DV]
                o = tot * lax.rsqrt(ms + RMS_EPS) * bon_ref[...] * (brh * jax.nn.sigmoid(brh))
            o_ref[rows, h * B_DV:(h + 1) * B_DV] = o
    sfin_ref[...] = st[...]


def _gla(lay, p, gw_pad, gb, s0, reverse, o_fwd=None, b_on=None):
    n_ctx_blocks = lay.nctx // ROW_BLOCK
    per_seq = lay.tl // ROW_BLOCK
    n_blocks = lay.nt // ROW_BLOCK

    def blk(i):
        j = i - n_ctx_blocks
        within = j % per_seq
        lat = n_ctx_blocks + (j - within) + (per_seq - 1 - within if reverse else within)
        return jnp.where(i < n_ctx_blocks, i, lat)

    def seq(i):
        return jnp.where(i < n_ctx_blocks, i, n_ctx_blocks + (i - n_ctx_blocks) // per_seq)

    def s0_idx(i):
        return jnp.where(i < n_ctx_blocks, lay.bl, (i - n_ctx_blocks) // per_seq)

    ri = jnp.arange(B_CHUNK)
    tri = ((ri[:, None] <= ri[None, :]) if reverse else (ri[:, None] >= ri[None, :])).astype(F32)
    in_specs = [pl.BlockSpec((ROW_BLOCK, 256), lambda i: (blk(i), E_BQ // 256)),
                pl.BlockSpec((ROW_BLOCK, 256), lambda i: (blk(i), E_BK // 256)),
                pl.BlockSpec((ROW_BLOCK, 512), lambda i: (blk(i), E_BV // 512)),
                pl.BlockSpec((ROW_BLOCK, LANE), lambda i: (blk(i), E_BG // LANE)),
                pl.BlockSpec((LANE, 256), lambda i: (0, 0)),
                pl.BlockSpec((1, 256), lambda i: (0, 0)),
                pl.BlockSpec((None, B_HEADS, LANE, LANE), lambda i: (s0_idx(i), 0, 0, 0)),
                pl.BlockSpec((B_CHUNK, B_CHUNK), lambda i: (0, 0))]
    args = [p, p, p, p, gw_pad, gb.reshape(1, 256), s0, tri]
    if reverse:
        in_specs += [pl.BlockSpec((ROW_BLOCK, 512), lambda i: (blk(i), 0)),
                     pl.BlockSpec((ROW_BLOCK, 512), lambda i: (blk(i), E_BR // 512)),
                     pl.BlockSpec((1, B_DV), lambda i: (0, 0))]
        args += [o_fwd, p, b_on.reshape(1, B_DV)]
    n_seq = n_ctx_blocks + lay.bl
    return pl.pallas_call(
        functools.partial(_gla_body, reverse=reverse, n_ctx_blocks=n_ctx_blocks, per_seq=per_seq),
        grid=(n_blocks,),
        in_specs=in_specs,
        out_specs=[pl.BlockSpec((ROW_BLOCK, 512), lambda i: (blk(i), 0)),
                   pl.BlockSpec((None, B_HEADS, LANE, LANE), lambda i: (seq(i), 0, 0, 0))],
        out_shape=[jax.ShapeDtypeStruct((lay.nt, 512), F32),
                   jax.ShapeDtypeStruct((n_seq, B_HEADS, LANE, LANE), F32)],
        scratch_shapes=[pltpu.VMEM((B_HEADS, LANE, LANE), F32)],
        compiler_params=_params(1),
        name="gla_bwd" if reverse else "gla_fwd",
    )(*args)


def _state_in(s):
    st = jnp.swapaxes(s.astype(F32), 2, 3)
    z = jnp.zeros_like(st)
    even = jnp.concatenate([st, z], axis=-1)
    odd = jnp.concatenate([z, st], axis=-1)
    pick = (jnp.arange(B_HEADS) % 2 == 0)[None, :, None, None]
    full = jnp.where(pick, even, odd)
    return jnp.concatenate([full, jnp.zeros_like(full[:1])], axis=0)


def _state_out(s_t):
    even = s_t[..., :B_DK]
    odd = s_t[..., B_DK:]
    pick = (jnp.arange(B_HEADS) % 2 == 0)[None, :, None, None]
    return jnp.swapaxes(jnp.where(pick, even, odd), 2, 3)


def _linear_res_body(*refs, n_in):
    a_refs, w_refs = refs[:n_in], refs[n_in:2 * n_in]
    gate_ref, res_ref, o_ref = refs[2 * n_in:]
    acc = _dot(a_refs[0][...], w_refs[0][...])
    for a_ref, w_ref in zip(a_refs[1:], w_refs[1:]):
        acc = acc + _dot(a_ref[...], w_ref[...])
    o_ref[...] = res_ref[...] + gate_ref[0] * acc


def _linear_res(lay, xs, ws, mod, chunk, res):
    n_in = len(xs)
    in_specs = [pl.BlockSpec((LIN_BLOCK, a.shape[1]), lambda i: (i, 0)) for a in xs]
    in_specs += [pl.BlockSpec(w.shape, lambda i: (0, 0)) for w in ws]
    in_specs += [_mod_spec(lay, LIN_BLOCK, chunk),
                 pl.BlockSpec((LIN_BLOCK, D_MODEL), lambda i: (i, 0))]
    return pl.pallas_call(
        functools.partial(_linear_res_body, n_in=n_in),
        grid=(lay.nt // LIN_BLOCK,),
        in_specs=in_specs,
        out_specs=pl.BlockSpec((LIN_BLOCK, D_MODEL), lambda i: (i, 0)),
        out_shape=jax.ShapeDtypeStruct((lay.nt, D_MODEL), F32),
        compiler_params=_params(1),
        name="linear_res",
    )(*xs, *ws, mod, res)


def _diff_attn_body(*refs, lam_init, has_cache):
    if has_cache:
        lamv_ref, con_ref, q_ref, k_ref, v_ref, kc_ref, vc_ref, o_ref = refs
    else:
        lamv_ref, con_ref, q_ref, k_ref, v_ref, o_ref = refs
    lv = lamv_ref[...]
    lam = (jnp.exp(jnp.sum(lv[0:1] * lv[1:2], axis=-1, keepdims=True))
           - jnp.exp(jnp.sum(lv[2:3] * lv[3:4], axis=-1, keepdims=True)) + lam_init)
    q = q_ref[...] * ATTN_SCALE
    tq = q.shape[0]
    lo = _lo_lanes(q.shape)
    qs = jnp.concatenate([jnp.where(lo, q, 0.0), jnp.where(lo, 0.0, q)], axis=0)
    s = _dot_nt(qs, k_ref[...])
    mx = jnp.max(s, axis=-1, keepdims=True)
    if has_cache:
        sc = _dot_nt(qs, kc_ref[...])
        mx = jnp.maximum(mx, jnp.max(sc, axis=-1, keepdims=True))
    p = jnp.exp(s - mx)
    den = jnp.sum(p, axis=-1, keepdims=True)
    if has_cache:
        pc = jnp.exp(sc - mx)
        den = den + jnp.sum(pc, axis=-1, keepdims=True)
    inv = 1.0 / den
    w0 = inv[:tq]
    w1 = lam * inv[tq:]
    o = _dot(p[:tq] * w0 - p[tq:] * w1, v_ref[...])
    if has_cache:
        o = o + _dot(pc[:tq] * w0 - pc[tq:] * w1, vc_ref[...])
    ms = jnp.mean(o * o, axis=-1, keepdims=True)
    o_ref[...] = o * lax.rsqrt(ms + RMS_EPS) * con_ref[...] * (1.0 - lam_init)


def _diff_attn(qk, p, lamv, c_on, lam_init, n_seq, t_seq, row_base, tq, cache=None):
    nq = t_seq // tq
    qbase = row_base // tq
    kbase = row_base // t_seq
    in_specs = [pl.BlockSpec((8, LANE), lambda b, h, i: (0, 0)),
                pl.BlockSpec((1, LANE), lambda b, h, i: (0, 0)),
                pl.BlockSpec((tq, LANE), lambda b, h, i: (qbase + b * nq + i, h)),
                pl.BlockSpec((t_seq, LANE), lambda b, h, i: (kbase + b, C_HEADS + h)),
                pl.BlockSpec((t_seq, LANE), lambda b, h, i: (kbase + b, 2 * C_HEADS + h))]
    args = [lamv, c_on.reshape(1, C_DV), qk, qk, p]
    if cache is not None:
        ck, cv, layer = cache
        lc = ck.shape[2]
        in_specs += [pl.BlockSpec((None, None, lc, LANE), lambda b, h, i: (b, layer, 0, h)),
                     pl.BlockSpec((None, None, lc, LANE), lambda b, h, i: (b, layer, 0, h))]
        args += [ck.reshape(ck.shape[:3] + (C_HEADS * LANE,)), cv.reshape(cv.shape[:3] + (C_HEADS * LANE,))]
    return pl.pallas_call(
        functools.partial(_diff_attn_body, lam_init=lam_init, has_cache=cache is not None),
        grid=(n_seq, C_HEADS, nq),
        in_specs=in_specs,
        out_specs=pl.BlockSpec((tq, LANE), lambda b, h, i: (b * nq + i, h)),
        out_shape=jax.ShapeDtypeStruct((n_seq * t_seq, C_HEADS * C_DV), F32),
        compiler_params=_params(3),
        name="diff_attn",
    )(*args)


def _top_rows(s, k, out_ref, want_rank=False):
    cur = s
    first = None
    rank = jnp.full(s.shape, float(k), F32) if want_rank else None
    for r in range(k):
        m = jnp.max(cur, axis=0, keepdims=True)
        out_ref[r:r + 1, :] = m
        first = m if first is None else first
        hit = cur == m
        if want_rank:
            rank = jnp.where(hit, float(r), rank)
        cur = jnp.where(hit, NEG_BIG, cur)
    return first, m, rank


_CAND_COUNTS = tuple(P_TOPK // (a + 1) for a in range(P_TOPK))
_CAND_ROWS = SUBLANES * ((sum(_CAND_COUNTS) + SUBLANES - 1) // SUBLANES)


def _peer_select(x_ref, g_ref, sc_ref, sh_ref, wq_ref, sk_ref,
                 h_s, q_s, r0_s, e0_s, n1_s, e1_s, top0_s, top1_s, cand_s):
    h = _rms_mod(x_ref[...], g_ref[...], sc_ref[0], sh_ref[0])
    h_s[...] = h.astype(BF16)
    q_s[...] = _dot(h, wq_ref[...])

    def head(hd, carry):
        qp = q_s[:, pl.ds(pl.multiple_of(hd * LANE, LANE), LANE)]
        s_t = lax.dot_general(sk_ref[hd], qp, (((1,), (1,)), ((), ())), precision=HI,
                              preferred_element_type=F32)
        for t in range(s_t.shape[1] // LANE):
            cols = slice(t * LANE, (t + 1) * LANE)
            s0 = s_t[:P_NKEYS, cols]
            s1 = s_t[P_NKEYS:, cols]
            max0, _, r0 = _top_rows(s0, P_TOPK, top0_s, want_rank=True)
            max1, _, _ = _top_rows(s1, P_TOPK, top1_s)
            row = 0
            for a, cnt in enumerate(_CAND_COUNTS):
                cand_s[row:row + cnt, :] = top0_s[a:a + 1, :] + top1_s[0:cnt, :]
                row += cnt
            cand_s[row:, :] = jnp.full((_CAND_ROWS - row, LANE), NEG_BIG, F32)
            cand = cand_s[...]
            _, thr, _ = _top_rows(cand, P_TOPK, top1_s)
            mx = max0 + max1
            z = jnp.sum(jnp.where(cand >= thr, jnp.exp(cand - mx), 0.0), axis=0, keepdims=True)
            n1 = jnp.zeros(s1.shape, F32)
            for a in range(P_TOPK):
                n1 = n1 + jnp.where((top0_s[a:a + 1, :] + s1) >= thr, 1.0, 0.0)
            r0_s[hd, :, cols] = r0
            e0_s[hd, :, cols] = jnp.exp(s0 - max0)
            n1_s[hd, t] = n1
            e1_s[hd, t] = jnp.exp(s1 - max1) / z
        return carry

    lax.fori_loop(0, P_HEADS, head, 0)


def _peer_step(group, u_ref, vt_ref, h_s, r0_s, e0_s, r0c_s, e0c_s, n1_s, e1_s,
               st_score, st_gate, wt_gate, wt_acc, acc_s):
    tn = h_s.shape[0]
    n_t = tn // LANE
    kc = u_ref.shape[0] // n_t
    r0c_s[...] = r0_s[:, group, :]
    e0c_s[...] = e0_s[:, group, :]
    for t in range(n_t):
        cols = slice(t * LANE, (t + 1) * LANE)
        for jq in range(P_NKEYS // GATE_ROWS):
            jr = slice(jq * GATE_ROWS, (jq + 1) * GATE_ROWS)
            g = [jnp.zeros((GATE_ROWS, LANE), F32) for _ in range(SUBLANES)]
            for hd in range(P_HEADS):
                n1q = n1_s[hd, t, jr, :]
                e1q = e1_s[hd, t, jr, :]
                for il in range(SUBLANES):
                    keep = r0c_s[hd, il:il + 1, cols] < n1q
                    g[il] = g[il] + e0c_s[hd, il:il + 1, cols] * jnp.where(keep, e1q, 0.0)
            for il in range(SUBLANES):
                rows = slice(il * P_NKEYS + jq * GATE_ROWS, il * P_NKEYS + (jq + 1) * GATE_ROWS)
                sc = st_gate[t, rows, :]
                act = 0.5 * sc * (1.0 + lax.erf(sc * (2.0 ** -0.5)))
                wt_gate[rows, cols] = (g[il] * act).astype(BF16)
        ks = slice(t * kc, (t + 1) * kc)
        acc_s[...] += _dot(vt_ref[:, ks], wt_acc[ks, :])
        if t % 2 == 0:
            s_new = _dot_nt(u_ref[...], h_s[t * LANE:(t + 2) * LANE, :])
            st_score[t] = s_new[:, :LANE]
            st_score[t + 1] = s_new[:, LANE:]


def _peer_body(x_ref, g_ref, sc_ref, sh_ref, gate_ref, wq_ref, sk_ref, u_ref, vt_ref, o_ref,
               h_s, q_s, r0_s, e0_s, r0c_s, e0c_s, n1_s, e1_s, top0_s, top1_s, cand_s,
               st0_s, st1_s, wt0_s, wt1_s, acc_s):
    e = pl.program_id(1)
    n_steps = pl.num_programs(1)
    n_eb = n_steps - 2
    assert u_ref.shape[0] == SUBLANES * P_NKEYS

    @pl.when(e == 0)
    def _():
        _peer_select(x_ref, g_ref, sc_ref, sh_ref, wq_ref, sk_ref,
                     h_s, q_s, r0_s, e0_s, n1_s, e1_s, top0_s, top1_s, cand_s)
        acc_s[...] = jnp.zeros_like(acc_s)
        wt0_s[...] = jnp.zeros_like(wt0_s)
        wt1_s[...] = jnp.zeros_like(wt1_s)
        st1_s[...] = jnp.zeros_like(st1_s)

    gate_block = jnp.clip(e - 1, 0, n_eb - 1)
    group = pl.ds(pl.multiple_of(gate_block * SUBLANES, SUBLANES), SUBLANES)
    st = (st0_s, st1_s)
    wt = (wt0_s, wt1_s)
    for slot in range(2):
        @pl.when(e % 2 == slot)
        def _():
            _peer_step(group, u_ref, vt_ref, h_s, r0_s, e0_s, r0c_s, e0c_s, n1_s, e1_s,
                       st[slot], st[1 - slot], wt[1 - slot], wt[slot], acc_s)

    @pl.when(e == n_steps - 1)
    def _():
        o_ref[...] = x_ref[...] + gate_ref[0] * acc_s[...].T


def _peer(lay, x, g, mod, wq, sk2, u, vt):
    tn, en = PEER_TOKENS, PEER_EXPERTS
    n_eb = u.shape[0] // en
    grid = (lay.nt // tn, n_eb + 2)
    head_scr = pltpu.VMEM((P_HEADS, P_NKEYS, tn), F32)
    rows_scr = pltpu.VMEM((P_HEADS, SUBLANES, tn), F32)
    tile_scr = pltpu.VMEM((P_HEADS, tn // LANE, P_NKEYS, LANE), F32)
    score_scr = pltpu.VMEM((tn // LANE, en, LANE), F32)
    gated_scr = pltpu.VMEM((en, tn), BF16)
    return pl.pallas_call(
        _peer_body,
        grid=grid,
        in_specs=[pl.BlockSpec((tn, D_MODEL), lambda i, e: (i, 0)),
                  pl.BlockSpec((1, D_MODEL), lambda i, e: (0, 0)),
                  _mod_spec(lay, tn, 4), _mod_spec(lay, tn, 3), _mod_spec(lay, tn, 5),
                  pl.BlockSpec((D_MODEL, P_HEADS * LANE), lambda i, e: (0, 0)),
                  pl.BlockSpec((P_HEADS, 2 * P_NKEYS, LANE), lambda i, e: (0, 0, 0)),
                  pl.BlockSpec((en, D_MODEL), lambda i, e: (jnp.minimum(e, n_eb - 1), 0)),
                  pl.BlockSpec((D_MODEL, en), lambda i, e: (0, jnp.clip(e - 2, 0, n_eb - 1)))],
        out_specs=pl.BlockSpec((tn, D_MODEL), lambda i, e: (i, 0)),
        out_shape=jax.ShapeDtypeStruct((lay.nt, D_MODEL), F32),
        scratch_shapes=[pltpu.VMEM((tn, D_MODEL), BF16),
                        pltpu.VMEM((tn, P_HEADS * LANE), F32),
                        head_scr, head_scr, rows_scr, rows_scr, tile_scr, tile_scr,
                        pltpu.VMEM((P_TOPK, LANE), F32),
                        pltpu.VMEM((P_TOPK, LANE), F32),
                        pltpu.VMEM((_CAND_ROWS, LANE), F32),
                        score_scr, score_scr, gated_scr, gated_scr,
                        pltpu.VMEM((D_MODEL, tn), F32)],
        compiler_params=_params(2),
        name="peer",
    )(x, g.reshape(1, D_MODEL), mod, mod, mod, wq, sk2, u, vt)


def _sub_key_blocks(sub_keys):
    z = jnp.zeros_like(sub_keys[:, 0])
    top = jnp.concatenate([sub_keys[:, 0], z], axis=-1)
    bot = jnp.concatenate([z, sub_keys[:, 1]], axis=-1)
    return jnp.concatenate([top, bot], axis=1)


def _lambda_init(layer):
    return 0.8 - 0.6 * math.exp(-0.3 * layer)


def _even_w_in(w):
    aq, ak, av, bq, bk, bv, br, bg = jnp.split(w, [512, 640, 768, 1024, 1280, 1792, 2304], axis=1)
    pad = jnp.zeros((w.shape[0], E_WIDTH - E_BG - bg.shape[1]), w.dtype)
    return jnp.concatenate([aq, ak, av, bq, bv, br, bk, bg, pad], axis=1).astype(BF16)


def kernel(x_prompt, x_sample, cache_a_k, cache_a_v, state_b_fwd, state_b_bwd, cache_c_k, cache_c_v, c, c_ctx, ada_w, ada_b, norm_mix_g, norm_ffn_g, e_w_in, e_w_out, a_q_norm, a_k_norm, a_sink, b_gate_w_f, b_gate_b_f, b_gate_w_b, b_gate_b_b, b_out_norm, o_w_in, o_w_out, c_q_norm, c_k_norm, c_lambda_q1, c_lambda_k1, c_lambda_q2, c_lambda_k2, c_out_norm, p_w_q, p_sub_keys, p_u, p_v):
    bc, tc, d = x_prompt.shape
    bl, tl, _ = x_sample.shape
    depth = ada_w.shape[0]
    lay = _Layout(bc, tc, bl, tl)
    n_mod_rows = 8 * ((bl + 1 + 7) // 8)
    cvecs = jnp.concatenate([c, c_ctx[None, :], jnp.zeros((n_mod_rows - bl - 1, d), F32)], axis=0)
    mods = _modulation(cvecs, ada_w, ada_b)
    cos, sin = _rope_tables(tl)
    x = jnp.concatenate([x_prompt.reshape(lay.nctx, d), x_sample.reshape(bl * tl, d)], axis=0)

    ak_l, av_l, sf_l, sb_l, ck_l, cv_l = [], [], [], [], [], []
    for l in range(depth):
        i = l // 2
        mod = mods[l].reshape(n_mod_rows, 1, N_MOD * d)
        if l % 2 == 0:
            p = _ln_proj(lay, x, norm_mix_g[l], mod, 0, _even_w_in(e_w_in[i]))
            gains = jnp.concatenate([jnp.tile(a_q_norm[i], A_HEADS), jnp.tile(a_k_norm[i], A_KV_HEADS)])
            qk = _qk_prep(lay, p, E_QK_WIDTH, gains.reshape(1, E_QK_WIDTH), cos, sin)
            sink = a_sink[i].reshape(A_HEADS)
            oa = jnp.concatenate([_attn_a_ctx(lay, qk, p, sink),
                                  _attn_a_lat(lay, qk, p, sink, cache_a_k, cache_a_v, i)], axis=0)
            zrow = jnp.zeros((LANE - 2 * B_GATE_RANK, 256), F32)
            zgate = jnp.zeros((B_GATE_RANK, 256), F32)
            gw_f = jnp.concatenate([b_gate_w_f[i], zgate, zrow], axis=0)
            gw_b = jnp.concatenate([zgate, b_gate_w_b[i], zrow], axis=0)
            o_f, s_f = _gla(lay, p, gw_f, b_gate_b_f[i], _state_in(state_b_fwd[:, i]), False)
            ob, s_b = _gla(lay, p, gw_b, b_gate_b_b[i], _state_in(state_b_bwd[:, i]), True,
                           o_fwd=o_f, b_on=b_out_norm[i])
            w_out = e_w_out[i].astype(BF16)
            x = _linear_res(lay, [oa, ob], [w_out[:512], w_out[512:]], mod, 2, x)
            ak_l.append(qk[:lay.nctx, E_AK:E_AK + LANE].reshape(bc, tc, A_KV_HEADS, HEAD_DIM))
            av_l.append(p[:lay.nctx, E_AV:E_AV + LANE].reshape(bc, tc, A_KV_HEADS, HEAD_DIM))
            sf_l.append(_state_out(s_f[:bc]))
            sb_l.append(_state_out(s_b[:bc]))
        else:
            lam_init = _lambda_init(l)
            p = _ln_proj(lay, x, norm_mix_g[l], mod, 0, o_w_in[i].astype(BF16))
            gains = jnp.concatenate([jnp.tile(c_q_norm[i], 2 * C_HEADS), jnp.tile(c_k_norm[i], 2 * C_HEADS)])
            qk = _qk_prep(lay, p, O_QK_WIDTH, gains.reshape(1, O_QK_WIDTH), cos, sin)
            lamv = jnp.stack([c_lambda_q1[i], c_lambda_k1[i], c_lambda_q2[i], c_lambda_k2[i]])
            lamv = jnp.pad(lamv, ((0, 4), (0, LANE - HEAD_DIM)))
            o_ctx = _diff_attn(qk, p, lamv, c_out_norm[i], lam_init, bc, tc, 0, tc)
            o_lat = _diff_attn(qk, p, lamv, c_out_norm[i], lam_init, bl, tl, lay.nctx, ATTN_Q_BLOCK,
                               cache=(cache_c_k, cache_c_v, i))
            o = jnp.concatenate([o_ctx, o_lat], axis=0)
            x = _linear_res(lay, [o], [o_w_out[i].astype(BF16)], mod, 2, x)
            ck_l.append(qk[:lay.nctx, 1024:2048].reshape(bc, tc, C_HEADS, 2, HEAD_DIM))
            cv_l.append(p[:lay.nctx, 2048:3072].reshape(bc, tc, C_HEADS, C_DV))
        x = _peer(lay, x, norm_ffn_g[l], mod, p_w_q[l].astype(BF16), _sub_key_blocks(p_sub_keys[l]),
                  p_u[l].astype(BF16), p_v[l].T.astype(BF16))

    return (x[:lay.nctx].reshape(bc, tc, d), x[lay.nctx:].reshape(bl, tl, d),
            jnp.stack(ak_l, axis=1), jnp.stack(av_l, axis=1), jnp.stack(sf_l, axis=1), jnp.stack(sb_l, axis=1),
            jnp.stack(ck_l, axis=1), jnp.stack(cv_l, axis=1))
```

```python
import functools
import math

import jax
import jax.numpy as jnp
from jax import lax
from jax.experimental import pallas as pl
from jax.experimental.pallas import tpu as pltpu

F32 = jnp.float32
BF16 = jnp.bfloat16
HI = lax.Precision.HIGHEST

D_MODEL = 1024
GRID_W = 64
HEAD_DIM = 64
ATTN_SCALE = HEAD_DIM ** -0.5
ROPE_BASE = 10000.0
RMS_EPS = 1e-6
N_MOD = 6
A_HEADS = 8
A_KV_HEADS = 2
A_WINDOW = 128
B_HEADS = 4
B_DK = 64
B_DV = 128
B_GATE_RANK = 16
B_GATE_TAU = 16.0
B_CHUNK = 64
C_HEADS = 8
C_DV = 128
P_HEADS = 8
P_NKEYS = 128
P_TOPK = 16

LANE = 128
SUBLANES = 8
ROW_BLOCK = 256
LIN_BLOCK = 512
PEER_TOKENS = 512
PEER_EXPERTS = SUBLANES * P_NKEYS
GATE_ROWS = 32
ATTN_Q_BLOCK = 128
DIFF_Q_BLOCK = 256
NEG_BIG = -1e30
VMEM_LIMIT = 56 * 1024 * 1024

E_AQ, E_AK, E_AV, E_BQ, E_BV, E_BR, E_BK, E_BG, E_WIDTH = 0, 512, 640, 768, 1024, 1536, 2048, 2304, 2432
E_QK_WIDTH = 640
O_QK_WIDTH = 2048


def _dot(a, b):
    return lax.dot_general(a.astype(BF16), b.astype(BF16), (((1,), (0,)), ((), ())),
                           preferred_element_type=F32)


def _dot_nt(a, b):
    return lax.dot_general(a.astype(BF16), b.astype(BF16), (((1,), (1,)), ((), ())),
                           preferred_element_type=F32)


def _dot_hi(a, b):
    return lax.dot_general(a, b, (((1,), (0,)), ((), ())), precision=HI, preferred_element_type=F32)


def _params(n_axes, flags=None):
    return pltpu.CompilerParams(dimension_semantics=("arbitrary",) * n_axes,
                                vmem_limit_bytes=VMEM_LIMIT, flags=flags)


def _lo_lanes(shape):
    return (lax.broadcasted_iota(jnp.int32, shape, len(shape) - 1) & 64) == 0


def _mod_body(c_ref, w_ref, b_ref, o_ref):
    c = c_ref[...]
    s = c * jax.nn.sigmoid(c)
    o_ref[0] = _dot_hi(s, w_ref[0]) + b_ref[0]


def _modulation(cvecs, ada_w, ada_b):
    depth, d, n = ada_w.shape
    r = cvecs.shape[0]
    tn = 1536
    return pl.pallas_call(
        _mod_body,
        grid=(depth, n // tn),
        in_specs=[pl.BlockSpec((r, d), lambda l, j: (0, 0)),
                  pl.BlockSpec((1, d, tn), lambda l, j: (l, 0, j)),
                  pl.BlockSpec((1, 1, tn), lambda l, j: (l, 0, j))],
        out_specs=pl.BlockSpec((1, r, tn), lambda l, j: (l, 0, j)),
        out_shape=jax.ShapeDtypeStruct((depth, r, n), F32),
        compiler_params=_params(2),
        name="modulation",
    )(cvecs, ada_w, ada_b.reshape(depth, 1, n))


class _Layout:
    def __init__(self, bc, tc, bl, tl):
        self.bc, self.tc, self.bl, self.tl = bc, tc, bl, tl
        self.nctx = bc * tc
        self.nt = self.nctx + bl * tl
        assert tc == ROW_BLOCK and tl % LIN_BLOCK == 0 and self.nctx % LIN_BLOCK == 0
        assert self.nctx % tl == 0 and tl % GRID_W == 0 and self.nt % PEER_TOKENS == 0
        assert self.nctx % PEER_TOKENS == 0 and tl % PEER_TOKENS == 0

    def mod_row(self, i, block):
        nctx_blocks = self.nctx // block
        return jnp.where(i < nctx_blocks, self.bl, (i - nctx_blocks) // (self.tl // block))


def _mod_spec(lay, block, chunk):
    return pl.BlockSpec((1, 1, D_MODEL), lambda i, *_: (lay.mod_row(i, block), 0, chunk))


def _rms_mod(x, g, sc, sh):
    ms = jnp.mean(x * x, axis=-1, keepdims=True)
    return (x * lax.rsqrt(ms + RMS_EPS) * g) * (1.0 + sc) + sh


def _ln_proj_body(x_ref, g_ref, sc_ref, sh_ref, w_ref, o_ref):
    h = _rms_mod(x_ref[...], g_ref[...], sc_ref[0], sh_ref[0])
    o_ref[...] = _dot(h, w_ref[...])


def _ln_proj(lay, x, g, mod, chunk, w):
    n_out = w.shape[1]
    return pl.pallas_call(
        _ln_proj_body,
        grid=(lay.nt // LIN_BLOCK,),
        in_specs=[pl.BlockSpec((LIN_BLOCK, D_MODEL), lambda i: (i, 0)),
                  pl.BlockSpec((1, D_MODEL), lambda i: (0, 0)),
                  _mod_spec(lay, LIN_BLOCK, chunk + 1),
                  _mod_spec(lay, LIN_BLOCK, chunk),
                  pl.BlockSpec((D_MODEL, n_out), lambda i: (0, 0))],
        out_specs=pl.BlockSpec((LIN_BLOCK, n_out), lambda i: (i, 0)),
        out_shape=jax.ShapeDtypeStruct((lay.nt, n_out), F32),
        compiler_params=_params(1),
        name="ln_proj",
    )(x, g.reshape(1, D_MODEL), mod, mod, w)


def _prep_body(x_ref, g_ref, cos_ref, sin_ref, bd_ref, o_ref, *, n_chunks):
    c = cos_ref[...]
    s = sin_ref[...]
    first = (lax.broadcasted_iota(jnp.int32, c.shape, 1) & 16) == 0
    for j in range(n_chunks):
        cols = slice(j * LANE, (j + 1) * LANE)
        x = x_ref[:, cols]
        ms = _dot_hi(x * x, bd_ref[...])
        y = x * lax.rsqrt(ms + RMS_EPS) * g_ref[:, cols]
        partner = jnp.where(first, pltpu.roll(y, LANE - 16, 1), pltpu.roll(y, 16, 1))
        o_ref[:, cols] = y * c + partner * s


def _rope_tables(tl):
    t = jnp.arange(tl)
    row = (t // GRID_W).astype(F32)
    col = (t % GRID_W).astype(F32)
    nf = HEAD_DIM // 4
    freqs = ROPE_BASE ** (-jnp.arange(nf, dtype=F32) / nf)
    ang_r = row[:, None] * freqs
    ang_c = col[:, None] * freqs
    cos64 = jnp.concatenate([jnp.cos(ang_r), jnp.cos(ang_r), jnp.cos(ang_c), jnp.cos(ang_c)], axis=1)
    sin64 = jnp.concatenate([-jnp.sin(ang_r), jnp.sin(ang_r), -jnp.sin(ang_c), jnp.sin(ang_c)], axis=1)
    cos = jnp.concatenate([jnp.ones((ROW_BLOCK, HEAD_DIM), F32), cos64], axis=0)
    sin = jnp.concatenate([jnp.zeros((ROW_BLOCK, HEAD_DIM), F32), sin64], axis=0)
    return jnp.tile(cos, (1, 2)), jnp.tile(sin, (1, 2))


def _qk_prep(lay, p, width, gains, cos, sin):
    n_ctx_blocks = lay.nctx // ROW_BLOCK
    per_seq = lay.tl // ROW_BLOCK

    def tab(i):
        return (jnp.where(i < n_ctx_blocks, 0, 1 + (i - n_ctx_blocks) % per_seq), 0)

    seg = jnp.arange(LANE) // HEAD_DIM
    bd = (seg[:, None] == seg[None, :]).astype(F32) / HEAD_DIM
    return pl.pallas_call(
        functools.partial(_prep_body, n_chunks=width // LANE),
        grid=(lay.nt // ROW_BLOCK,),
        in_specs=[pl.BlockSpec((ROW_BLOCK, width), lambda i: (i, 0)),
                  pl.BlockSpec((1, width), lambda i: (0, 0)),
                  pl.BlockSpec((ROW_BLOCK, LANE), tab),
                  pl.BlockSpec((ROW_BLOCK, LANE), tab),
                  pl.BlockSpec((LANE, LANE), lambda i: (0, 0))],
        out_specs=pl.BlockSpec((ROW_BLOCK, width), lambda i: (i, 0)),
        out_shape=jax.ShapeDtypeStruct((lay.nt, width), F32),
        compiler_params=_params(1),
        name="qk_prep",
    )(p, gains, cos, sin, bd)


def _dup_kv(x):
    lo = _lo_lanes(x.shape)
    xr = pltpu.roll(x, HEAD_DIM, 1)
    return jnp.where(lo, x, xr), jnp.where(lo, xr, x)


def _gqa_heads(sink_ref, q_ref, k2, v2, bias, o_ref):
    rows = q_ref.shape[0]
    lo = _lo_lanes((rows, LANE))
    for pair in range(A_HEADS // 2):
        kv = pair // 2
        cols = slice(pair * LANE, (pair + 1) * LANE)
        qp = q_ref[:, cols] * ATTN_SCALE
        outs = []
        for half in range(2):
            qm = jnp.where(lo, qp, 0.0) if half == 0 else jnp.where(lo, 0.0, qp)
            s = _dot_nt(qm, k2[kv])
            if bias is not None:
                s = s + bias
            sk = sink_ref[2 * pair + half]
            mx = jnp.maximum(jnp.max(s, axis=-1, keepdims=True), sk)
            p = jnp.exp(s - mx)
            den = jnp.sum(p, axis=-1, keepdims=True) + jnp.exp(sk - mx)
            outs.append(_dot(p, v2[kv]) / den)
        o_ref[:, cols] = jnp.where(lo, outs[0], outs[1])


def _attn_a_ctx_body(sink_ref, q_ref, k_ref, v_ref, o_ref):
    _gqa_heads(sink_ref, q_ref, _dup_kv(k_ref[...]), _dup_kv(v_ref[...]), None, o_ref)


def _attn_a_ctx(lay, qk, p, sink):
    return pl.pallas_call(
        _attn_a_ctx_body,
        grid=(lay.bc,),
        in_specs=[pl.BlockSpec(memory_space=pltpu.SMEM),
                  pl.BlockSpec((lay.tc, 512), lambda b: (b, 0)),
                  pl.BlockSpec((lay.tc, LANE), lambda b: (b, E_AK // LANE)),
                  pl.BlockSpec((lay.tc, LANE), lambda b: (b, E_AV // LANE))],
        out_specs=pl.BlockSpec((lay.tc, 512), lambda b: (b, 0)),
        out_shape=jax.ShapeDtypeStruct((lay.nctx, 512), F32),
        compiler_params=_params(1),
        name="attn_a_ctx",
    )(sink, qk, qk, p)


def _attn_a_lat_body(sink_ref, q_ref, k0, k1, k2r, v0, v1, v2r, kc_ref, vc_ref, o_ref):
    n = pl.program_id(1)
    last = pl.num_programs(1) - 1
    w = ATTN_Q_BLOCK
    lc = kc_ref.shape[0]
    k_all = jnp.concatenate([k0[...], k1[...], k2r[...], kc_ref[...]], axis=0)
    v_all = jnp.concatenate([v0[...], v1[...], v2r[...], vc_ref[...]], axis=0)
    i = lax.broadcasted_iota(jnp.int32, (w, 3 * w + lc), 0)
    j = lax.broadcasted_iota(jnp.int32, (w, 3 * w + lc), 1)
    valid = (jnp.abs(i + w - j) <= A_WINDOW) & ((j >= w) | (n > 0)) & ((j < 2 * w) | (n < last))
    bias = jnp.where(valid | (j >= 3 * w), 0.0, NEG_BIG)
    _gqa_heads(sink_ref, q_ref, _dup_kv(k_all), _dup_kv(v_all), bias, o_ref)


def _attn_a_lat(lay, qk, p, sink, cache_k, cache_v, layer):
    w = ATTN_Q_BLOCK
    nb = lay.tl // w
    base = lay.nctx // w
    lc = cache_k.shape[2]

    def row(off):
        return lambda b, n: base + b * nb + jnp.clip(n + off, 0, nb - 1)

    def kspec(off):
        r = row(off)
        return pl.BlockSpec((w, LANE), lambda b, n: (r(b, n), E_AK // LANE))

    def vspec(off):
        r = row(off)
        return pl.BlockSpec((w, LANE), lambda b, n: (r(b, n), E_AV // LANE))

    cspec = pl.BlockSpec((None, None, lc, LANE), lambda b, n: (b, layer, 0, 0))
    r0 = row(0)
    return pl.pallas_call(
        _attn_a_lat_body,
        grid=(lay.bl, nb),
        in_specs=[pl.BlockSpec(memory_space=pltpu.SMEM),
                  pl.BlockSpec((w, 512), lambda b, n: (r0(b, n), 0)),
                  kspec(-1), kspec(0), kspec(1), vspec(-1), vspec(0), vspec(1), cspec, cspec],
        out_specs=pl.BlockSpec((w, 512), lambda b, n: (b * nb + n, 0)),
        out_shape=jax.ShapeDtypeStruct((lay.bl * lay.tl, 512), F32),
        compiler_params=_params(2),
        name="attn_a_lat",
    )(sink, qk, qk, qk, qk, p, p, p,
      cache_k.reshape(cache_k.shape[:3] + (LANE,)), cache_v.reshape(cache_v.shape[:3] + (LANE,)))


def _gla_body(*refs, reverse, n_ctx_blocks, per_seq):
    if reverse:
        (bq_ref, bk_ref, bv_ref, bg_ref, gw_ref, gb_ref, s0_ref, tri_ref, of_ref, br_ref, bon_ref,
         o_ref, sfin_ref, st) = refs
    else:
        bq_ref, bk_ref, bv_ref, bg_ref, gw_ref, gb_ref, s0_ref, tri_ref, o_ref, sfin_ref, st = refs
    i = pl.program_id(0)
    is_start = (i < n_ctx_blocks) | (((i - n_ctx_blocks) % per_seq) == 0)

    @pl.when(is_start)
    def _():
        st[...] = s0_ref[...]

    z = _dot_hi(bg_ref[...], gw_ref[...]) + gb_ref[...]
    lg = (jnp.minimum(z, 0.0) - jnp.log1p(jnp.exp(-jnp.abs(z)))) * (1.0 / B_GATE_TAU)
    q = bq_ref[...] * (B_DK ** -0.5)
    k = bk_ref[...]
    tri = tri_ref[...]
    L = B_CHUNK
    n_chunks = ROW_BLOCK // L
    lo = _lo_lanes((L, LANE))
    ri = lax.broadcasted_iota(jnp.int32, (L, L), 0)
    ci = lax.broadcasted_iota(jnp.int32, (L, L), 1)
    causal = (ri <= ci) if reverse else (ri >= ci)
    order = range(n_chunks - 1, -1, -1) if reverse else range(n_chunks)
    for c in order:
        rows = slice(c * L, (c + 1) * L)
        b = _dot_hi(tri, lg[rows])
        b_last = b[0:1] if reverse else b[L - 1:L]
        qd = q[rows] * jnp.exp(b)
        kd = k[rows] * jnp.exp(-b)
        kt = k[rows] * jnp.exp(b_last - b)
        dec = jnp.exp(b_last)
        for h in range(B_HEADS):
            cols = slice((h // 2) * LANE, (h // 2 + 1) * LANE)
            mine = lo if h % 2 == 0 else jnp.logical_not(lo)
            qm = jnp.where(mine, qd[:, cols], 0.0)
            a = jnp.where(causal, _dot_nt(qm, kd[:, cols]), 0.0)
            vh = bv_ref[rows, h * B_DV:(h + 1) * B_DV]
            s_t = st[h]
            o = _dot(a, vh) + _dot_nt(qm, s_t)
            kv_t = _dot(vh.T, jnp.where(mine, kt[:, cols], 0.0))
            st[h] = s_t * dec[:, cols] + kv_t
            if reverse:
                tot = of_ref[rows, h * B_DV:(h + 1) * B_DV] + o
                ms = jnp.mean(tot * tot, axis=-1, keepdims=True)
                brh = br_ref[rows, h * B_DV:(h + 1) * B_DV]
                o = tot * lax.rsqrt(ms + RMS_EPS) * bon_ref[...] * (brh * jax.nn.sigmoid(brh))
            o_ref[rows, h * B_DV:(h + 1) * B_DV] = o
    sfin_ref[...] = st[...]


def _gla(lay, p, gw_pad, gb, s0, reverse, o_fwd=None, b_on=None):
    n_ctx_blocks = lay.nctx // ROW_BLOCK
    per_seq = lay.tl // ROW_BLOCK
    n_blocks = lay.nt // ROW_BLOCK

    def blk(i):
        j = i - n_ctx_blocks
        within = j % per_seq
        lat = n_ctx_blocks + (j - within) + (per_seq - 1 - within if reverse else within)
        return jnp.where(i < n_ctx_blocks, i, lat)

    def seq(i):
        return jnp.where(i < n_ctx_blocks, i, n_ctx_blocks + (i - n_ctx_blocks) // per_seq)

    def s0_idx(i):
        return jnp.where(i < n_ctx_blocks, lay.bl, (i - n_ctx_blocks) // per_seq)

    ri = jnp.arange(B_CHUNK)
    tri = ((ri[:, None] <= ri[None, :]) if reverse else (ri[:, None] >= ri[None, :])).astype(F32)
    in_specs = [pl.BlockSpec((ROW_BLOCK, 256), lambda i: (blk(i), E_BQ // 256)),
                pl.BlockSpec((ROW_BLOCK, 256), lambda i: (blk(i), E_BK // 256)),
                pl.BlockSpec((ROW_BLOCK, 512), lambda i: (blk(i), E_BV // 512)),
                pl.BlockSpec((ROW_BLOCK, LANE), lambda i: (blk(i), E_BG // LANE)),
                pl.BlockSpec((LANE, 256), lambda i: (0, 0)),
                pl.BlockSpec((1, 256), lambda i: (0, 0)),
                pl.BlockSpec((None, B_HEADS, LANE, LANE), lambda i: (s0_idx(i), 0, 0, 0)),
                pl.BlockSpec((B_CHUNK, B_CHUNK), lambda i: (0, 0))]
    args = [p, p, p, p, gw_pad, gb.reshape(1, 256), s0, tri]
    if reverse:
        in_specs += [pl.BlockSpec((ROW_BLOCK, 512), lambda i: (blk(i), 0)),
                     pl.BlockSpec((ROW_BLOCK, 512), lambda i: (blk(i), E_BR // 512)),
                     pl.BlockSpec((1, B_DV), lambda i: (0, 0))]
        args += [o_fwd, p, b_on.reshape(1, B_DV)]
    n_seq = n_ctx_blocks + lay.bl
    return pl.pallas_call(
        functools.partial(_gla_body, reverse=reverse, n_ctx_blocks=n_ctx_blocks, per_seq=per_seq),
        grid=(n_blocks,),
        in_specs=in_specs,
        out_specs=[pl.BlockSpec((ROW_BLOCK, 512), lambda i: (blk(i), 0)),
                   pl.BlockSpec((None, B_HEADS, LANE, LANE), lambda i: (seq(i), 0, 0, 0))],
        out_shape=[jax.ShapeDtypeStruct((lay.nt, 512), F32),
                   jax.ShapeDtypeStruct((n_seq, B_HEADS, LANE, LANE), F32)],
        scratch_shapes=[pltpu.VMEM((B_HEADS, LANE, LANE), F32)],
        compiler_params=_params(1),
        name="gla_bwd" if reverse else "gla_fwd",
    )(*args)


def _state_in(s):
    st = jnp.swapaxes(s.astype(F32), 2, 3)
    z = jnp.zeros_like(st)
    even = jnp.concatenate([st, z], axis=-1)
    odd = jnp.concatenate([z, st], axis=-1)
    pick = (jnp.arange(B_HEADS) % 2 == 0)[None, :, None, None]
    full = jnp.where(pick, even, odd)
    return jnp.concatenate([full, jnp.zeros_like(full[:1])], axis=0)


def _state_out(s_t):
    even = s_t[..., :B_DK]
    odd = s_t[..., B_DK:]
    pick = (jnp.arange(B_HEADS) % 2 == 0)[None, :, None, None]
    return jnp.swapaxes(jnp.where(pick, even, odd), 2, 3)


def _linear_res_body(*refs, n_in):
    a_refs, w_refs = refs[:n_in], refs[n_in:2 * n_in]
    gate_ref, res_ref, o_ref = refs[2 * n_in:]
    acc = _dot(a_refs[0][...], w_refs[0][...])
    for a_ref, w_ref in zip(a_refs[1:], w_refs[1:]):
        acc = acc + _dot(a_ref[...], w_ref[...])
    o_ref[...] = res_ref[...] + gate_ref[0] * acc


def _linear_res(lay, xs, ws, mod, chunk, res):
    n_in = len(xs)
    in_specs = [pl.BlockSpec((LIN_BLOCK, a.shape[1]), lambda i: (i, 0)) for a in xs]
    in_specs += [pl.BlockSpec(w.shape, lambda i: (0, 0)) for w in ws]
    in_specs += [_mod_spec(lay, LIN_BLOCK, chunk),
                 pl.BlockSpec((LIN_BLOCK, D_MODEL), lambda i: (i, 0))]
    return pl.pallas_call(
        functools.partial(_linear_res_body, n_in=n_in),
        grid=(lay.nt // LIN_BLOCK,),
        in_specs=in_specs,
        out_specs=pl.BlockSpec((LIN_BLOCK, D_MODEL), lambda i: (i, 0)),
        out_shape=jax.ShapeDtypeStruct((lay.nt, D_MODEL), F32),
        compiler_params=_params(1),
        name="linear_res",
    )(*xs, *ws, mod, res)


def _diff_attn_body(*refs, lam_init, has_cache):
    if has_cache:
        lamv_ref, con_ref, q_ref, k_ref, v_ref, kc_ref, vc_ref, o_ref = refs
    else:
        lamv_ref, con_ref, q_ref, k_ref, v_ref, o_ref = refs
    lv = lamv_ref[...]
    lam = (jnp.exp(jnp.sum(lv[0:1] * lv[1:2], axis=-1, keepdims=True))
           - jnp.exp(jnp.sum(lv[2:3] * lv[3:4], axis=-1, keepdims=True)) + lam_init)
    q = q_ref[...] * ATTN_SCALE
    tq = q.shape[0]
    lo = _lo_lanes(q.shape)
    qs = jnp.concatenate([jnp.where(lo, q, 0.0), jnp.where(lo, 0.0, q)], axis=0)
    s = _dot_nt(qs, k_ref[...])
    mx = jnp.max(s, axis=-1, keepdims=True)
    if has_cache:
        sc = _dot_nt(qs, kc_ref[...])
        mx = jnp.maximum(mx, jnp.max(sc, axis=-1, keepdims=True))
    p = jnp.exp(s - mx)
    den = jnp.sum(p, axis=-1, keepdims=True)
    if has_cache:
        pc = jnp.exp(sc - mx)
        den = den + jnp.sum(pc, axis=-1, keepdims=True)
    inv = 1.0 / den
    w0 = inv[:tq]
    w1 = lam * inv[tq:]
    o = _dot(p[:tq] * w0 - p[tq:] * w1, v_ref[...])
    if has_cache:
        o = o + _dot(pc[:tq] * w0 - pc[tq:] * w1, vc_ref[...])
    ms = jnp.mean(o * o, axis=-1, keepdims=True)
    o_ref[...] = o * lax.rsqrt(ms + RMS_EPS) * con_ref[...] * (1.0 - lam_init)


def _diff_attn(qk, p, lamv, c_on, lam_init, n_seq, t_seq, row_base, tq, cache=None):
    nq = t_seq // tq
    qbase = row_base // tq
    kbase = row_base // t_seq
    in_specs = [pl.BlockSpec((8, LANE), lambda b, h, i: (0, 0)),
                pl.BlockSpec((1, LANE), lambda b, h, i: (0, 0)),
                pl.BlockSpec((tq, LANE), lambda b, h, i: (qbase + b * nq + i, h)),
                pl.BlockSpec((t_seq, LANE), lambda b, h, i: (kbase + b, C_HEADS + h)),
                pl.BlockSpec((t_seq, LANE), lambda b, h, i: (kbase + b, 2 * C_HEADS + h))]
    args = [lamv, c_on.reshape(1, C_DV), qk, qk, p]
    if cache is not None:
        ck, cv, layer = cache
        lc = ck.shape[2]
        in_specs += [pl.BlockSpec((None, None, lc, LANE), lambda b, h, i: (b, layer, 0, h)),
                     pl.BlockSpec((None, None, lc, LANE), lambda b, h, i: (b, layer, 0, h))]
        args += [ck.reshape(ck.shape[:3] + (C_HEADS * LANE,)), cv.reshape(cv.shape[:3] + (C_HEADS * LANE,))]
    return pl.pallas_call(
        functools.partial(_diff_attn_body, lam_init=lam_init, has_cache=cache is not None),
        grid=(n_seq, C_HEADS, nq),
        in_specs=in_specs,
        out_specs=pl.BlockSpec((tq, LANE), lambda b, h, i: (b * nq + i, h)),
        out_shape=jax.ShapeDtypeStruct((n_seq * t_seq, C_HEADS * C_DV), F32),
        compiler_params=_params(3),
        name="diff_attn",
    )(*args)


def _top_rows(s, k, out_ref, want_rank=False):
    cur = s
    first = None
    rank = jnp.full(s.shape, float(k), F32) if want_rank else None
    for r in range(k):
        m = jnp.max(cur, axis=0, keepdims=True)
        out_ref[r:r + 1, :] = m
        first = m if first is None else first
        hit = cur == m
        if want_rank:
            rank = jnp.where(hit, float(r), rank)
        cur = jnp.where(hit, NEG_BIG, cur)
    return first, m, rank


_CAND_COUNTS = tuple(P_TOPK // (a + 1) for a in range(P_TOPK))
_CAND_ROWS = SUBLANES * ((sum(_CAND_COUNTS) + SUBLANES - 1) // SUBLANES)


def _peer_select(x_ref, g_ref, sc_ref, sh_ref, wq_ref, sk_ref,
                 h_s, q_s, r0_s, e0_s, n1_s, e1_s, top0_s, top1_s, cand_s):
    h = _rms_mod(x_ref[...], g_ref[...], sc_ref[0], sh_ref[0])
    h_s[...] = h.T.astype(BF16)
    q_s[...] = _dot(h, wq_ref[...])

    def head(hd, carry):
        qp = q_s[:, pl.ds(pl.multiple_of(hd * LANE, LANE), LANE)]
        s_t = lax.dot_general(sk_ref[hd], qp, (((1,), (1,)), ((), ())), precision=HI,
                              preferred_element_type=F32)
        for t in range(s_t.shape[1] // LANE):
            cols = slice(t * LANE, (t + 1) * LANE)
            s0 = s_t[:P_NKEYS, cols]
            s1 = s_t[P_NKEYS:, cols]
            max0, _, r0 = _top_rows(s0, P_TOPK, top0_s, want_rank=True)
            max1, _, _ = _top_rows(s1, P_TOPK, top1_s)
            row = 0
            for a, cnt in enumerate(_CAND_COUNTS):
                cand_s[row:row + cnt, :] = top0_s[a:a + 1, :] + top1_s[0:cnt, :]
                row += cnt
            cand_s[row:, :] = jnp.full((_CAND_ROWS - row, LANE), NEG_BIG, F32)
            cand = cand_s[...]
            _, thr, _ = _top_rows(cand, P_TOPK, top1_s)
            mx = max0 + max1
            z = jnp.sum(jnp.where(cand >= thr, jnp.exp(cand - mx), 0.0), axis=0, keepdims=True)
            n1 = jnp.zeros(s1.shape, F32)
            for a in range(P_TOPK):
                n1 = n1 + jnp.where((top0_s[a:a + 1, :] + s1) >= thr, 1.0, 0.0)
            r0_s[hd, :, cols] = r0
            e0_s[hd, :, cols] = jnp.exp(s0 - max0)
            n1_s[hd, t] = n1.astype(BF16)
            e1_s[hd, t] = (jnp.exp(s1 - max1) / z).astype(BF16)
        return carry

    lax.fori_loop(0, P_HEADS, head, 0)


def _bf16_pair_words(x):
    bits = lax.bitcast_convert_type(x.astype(BF16).astype(F32), jnp.uint32)
    return bits | (bits >> 16)


def _packed_row(w):
    return pltpu.bitcast(jnp.broadcast_to(w, (GATE_ROWS // 2, LANE)), BF16)


def _peer_step(group, u_ref, vt_ref, h_s, r0_s, e0_s, r0c_s, e0c_s, n1_s, e1_s,
               st_score, st_gate, wt_gate, wt_acc, acc_s):
    tn = h_s.shape[1]
    n_t = tn // LANE
    kc = u_ref.shape[0] // n_t
    r0c_s[...] = _bf16_pair_words(r0_s[:, group, :])
    e0c_s[...] = _bf16_pair_words(e0_s[:, group, :])
    zero = jnp.zeros((GATE_ROWS, LANE), BF16)
    for t in range(n_t):
        cols = slice(t * LANE, (t + 1) * LANE)
        for jq in range(P_NKEYS // GATE_ROWS):
            jr = slice(jq * GATE_ROWS, (jq + 1) * GATE_ROWS)
            g = [zero for _ in range(SUBLANES)]
            for hd in range(P_HEADS):
                n1q = n1_s[hd, t, jr, :]
                e1q = e1_s[hd, t, jr, :]
                for il in range(SUBLANES):
                    r0r = _packed_row(r0c_s[hd, il:il + 1, cols])
                    e0r = _packed_row(e0c_s[hd, il:il + 1, cols])
                    g[il] = g[il] + e0r * jnp.where(r0r < n1q, e1q, zero)
            for il in range(SUBLANES):
                rows = slice(il * P_NKEYS + jq * GATE_ROWS, il * P_NKEYS + (jq + 1) * GATE_ROWS)
                sc = st_gate[t, rows, :]
                act = 0.5 * sc * (1.0 + lax.erf(sc * (2.0 ** -0.5)))
                wt_gate[rows, cols] = (g[il].astype(F32) * act).astype(BF16)
        ks = slice(t * kc, (t + 1) * kc)
        acc_s[...] += _dot(vt_ref[:, ks], wt_acc[ks, :])
        if t % 2 == 0:
            s_new = _dot(u_ref[...], h_s[:, t * LANE:(t + 2) * LANE])
            st_score[t] = s_new[:, :LANE]
            st_score[t + 1] = s_new[:, LANE:]


def _peer_body(x_ref, g_ref, sc_ref, sh_ref, gate_ref, wq_ref, sk_ref, u_ref, vt_ref, o_ref,
               h_s, q_s, r0_s, e0_s, r0c_s, e0c_s, n1_s, e1_s, top0_s, top1_s, cand_s,
               st0_s, st1_s, wt0_s, wt1_s, acc_s):
    e = pl.program_id(1)
    n_steps = pl.num_programs(1)
    n_eb = n_steps - 2
    assert u_ref.shape[0] == SUBLANES * P_NKEYS

    @pl.when(e == 0)
    def _():
        _peer_select(x_ref, g_ref, sc_ref, sh_ref, wq_ref, sk_ref,
                     h_s, q_s, r0_s, e0_s, n1_s, e1_s, top0_s, top1_s, cand_s)
        acc_s[...] = jnp.zeros_like(acc_s)
        wt0_s[...] = jnp.zeros_like(wt0_s)
        wt1_s[...] = jnp.zeros_like(wt1_s)
        st1_s[...] = jnp.zeros_like(st1_s)

    gate_block = jnp.clip(e - 1, 0, n_eb - 1)
    group = pl.ds(pl.multiple_of(gate_block * SUBLANES, SUBLANES), SUBLANES)
    st = (st0_s, st1_s)
    wt = (wt0_s, wt1_s)
    for slot in range(2):
        @pl.when(e % 2 == slot)
        def _():
            _peer_step(group, u_ref, vt_ref, h_s, r0_s, e0_s, r0c_s, e0c_s, n1_s, e1_s,
                       st[slot], st[1 - slot], wt[1 - slot], wt[slot], acc_s)

    @pl.when(e == n_steps - 1)
    def _():
        o_ref[...] = x_ref[...] + gate_ref[0] * acc_s[...].T


def _peer(lay, x, g, mod, wq, sk2, u, vt):
    tn, en = PEER_TOKENS, PEER_EXPERTS
    n_eb = u.shape[0] // en
    grid = (lay.nt // tn, n_eb + 2)
    head_scr = pltpu.VMEM((P_HEADS, P_NKEYS, tn), F32)
    rows_scr = pltpu.VMEM((P_HEADS, SUBLANES, tn), jnp.uint32)
    tile_scr = pltpu.VMEM((P_HEADS, tn // LANE, P_NKEYS, LANE), BF16)
    score_scr = pltpu.VMEM((tn // LANE, en, LANE), F32)
    gated_scr = pltpu.VMEM((en, tn), BF16)
    return pl.pallas_call(
        _peer_body,
        grid=grid,
        in_specs=[pl.BlockSpec((tn, D_MODEL), lambda i, e: (i, 0)),
                  pl.BlockSpec((1, D_MODEL), lambda i, e: (0, 0)),
                  _mod_spec(lay, tn, 4), _mod_spec(lay, tn, 3), _mod_spec(lay, tn, 5),
                  pl.BlockSpec((D_MODEL, P_HEADS * LANE), lambda i, e: (0, 0)),
                  pl.BlockSpec((P_HEADS, 2 * P_NKEYS, LANE), lambda i, e: (0, 0, 0)),
                  pl.BlockSpec((en, D_MODEL), lambda i, e: (jnp.minimum(e, n_eb - 1), 0)),
                  pl.BlockSpec((D_MODEL, en), lambda i, e: (0, jnp.clip(e - 2, 0, n_eb - 1)))],
        out_specs=pl.BlockSpec((tn, D_MODEL), lambda i, e: (i, 0)),
        out_shape=jax.ShapeDtypeStruct((lay.nt, D_MODEL), F32),
        scratch_shapes=[pltpu.VMEM((D_MODEL, tn), BF16),
                        pltpu.VMEM((tn, P_HEADS * LANE), F32),
                        head_scr, head_scr, rows_scr, rows_scr, tile_scr, tile_scr,
                        pltpu.VMEM((P_TOPK, LANE), F32),
                        pltpu.VMEM((P_TOPK, LANE), F32),
                        pltpu.VMEM((_CAND_ROWS, LANE), F32),
                        score_scr, score_scr, gated_scr, gated_scr,
                        pltpu.VMEM((D_MODEL, tn), F32)],
        compiler_params=_params(2),
        name="peer",
    )(x, g.reshape(1, D_MODEL), mod, mod, mod, wq, sk2, u, vt)


def _sub_key_blocks(sub_keys):
    z = jnp.zeros_like(sub_keys[:, 0])
    top = jnp.concatenate([sub_keys[:, 0], z], axis=-1)
    bot = jnp.concatenate([z, sub_keys[:, 1]], axis=-1)
    return jnp.concatenate([top, bot], axis=1)


def _lambda_init(layer):
    return 0.8 - 0.6 * math.exp(-0.3 * layer)


def _even_w_in(w):
    aq, ak, av, bq, bk, bv, br, bg = jnp.split(w, [512, 640, 768, 1024, 1280, 1792, 2304], axis=1)
    pad = jnp.zeros((w.shape[0], E_WIDTH - E_BG - bg.shape[1]), w.dtype)
    return jnp.concatenate([aq, ak, av, bq, bv, br, bk, bg, pad], axis=1).astype(BF16)


def kernel(x_prompt, x_sample, cache_a_k, cache_a_v, state_b_fwd, state_b_bwd, cache_c_k, cache_c_v, c, c_ctx, ada_w, ada_b, norm_mix_g, norm_ffn_g, e_w_in, e_w_out, a_q_norm, a_k_norm, a_sink, b_gate_w_f, b_gate_b_f, b_gate_w_b, b_gate_b_b, b_out_norm, o_w_in, o_w_out, c_q_norm, c_k_norm, c_lambda_q1, c_lambda_k1, c_lambda_q2, c_lambda_k2, c_out_norm, p_w_q, p_sub_keys, p_u, p_v):
    bc, tc, d = x_prompt.shape
    bl, tl, _ = x_sample.shape
    depth = ada_w.shape[0]
    lay = _Layout(bc, tc, bl, tl)
    n_mod_rows = 8 * ((bl + 1 + 7) // 8)
    cvecs = jnp.concatenate([c, c_ctx[None, :], jnp.zeros((n_mod_rows - bl - 1, d), F32)], axis=0)
    mods = _modulation(cvecs, ada_w, ada_b)
    cos, sin = _rope_tables(tl)
    x = jnp.concatenate([x_prompt.reshape(lay.nctx, d), x_sample.reshape(bl * tl, d)], axis=0)

    ak_l, av_l, sf_l, sb_l, ck_l, cv_l = [], [], [], [], [], []
    for l in range(depth):
        i = l // 2
        mod = mods[l].reshape(n_mod_rows, 1, N_MOD * d)
        if l % 2 == 0:
            p = _ln_proj(lay, x, norm_mix_g[l], mod, 0, _even_w_in(e_w_in[i]))
            gains = jnp.concatenate([jnp.tile(a_q_norm[i], A_HEADS), jnp.tile(a_k_norm[i], A_KV_HEADS)])
            qk = _qk_prep(lay, p, E_QK_WIDTH, gains.reshape(1, E_QK_WIDTH), cos, sin)
            sink = a_sink[i].reshape(A_HEADS)
            oa = jnp.concatenate([_attn_a_ctx(lay, qk, p, sink),
                                  _attn_a_lat(lay, qk, p, sink, cache_a_k, cache_a_v, i)], axis=0)
            zrow = jnp.zeros((LANE - 2 * B_GATE_RANK, 256), F32)
            zgate = jnp.zeros((B_GATE_RANK, 256), F32)
            gw_f = jnp.concatenate([b_gate_w_f[i], zgate, zrow], axis=0)
            gw_b = jnp.concatenate([zgate, b_gate_w_b[i], zrow], axis=0)
            o_f, s_f = _gla(lay, p, gw_f, b_gate_b_f[i], _state_in(state_b_fwd[:, i]), False)
            ob, s_b = _gla(lay, p, gw_b, b_gate_b_b[i], _state_in(state_b_bwd[:, i]), True,
                           o_fwd=o_f, b_on=b_out_norm[i])
            w_out = e_w_out[i].astype(BF16)
            x = _linear_res(lay, [oa, ob], [w_out[:512], w_out[512:]], mod, 2, x)
            ak_l.append(qk[:lay.nctx, E_AK:E_AK + LANE].reshape(bc, tc, A_KV_HEADS, HEAD_DIM))
            av_l.append(p[:lay.nctx, E_AV:E_AV + LANE].reshape(bc, tc, A_KV_HEADS, HEAD_DIM))
            sf_l.append(_state_out(s_f[:bc]))
            sb_l.append(_state_out(s_b[:bc]))
        else:
            lam_init = _lambda_init(l)
            p = _ln_proj(lay, x, norm_mix_g[l], mod, 0, o_w_in[i].astype(BF16))
            gains = jnp.concatenate([jnp.tile(c_q_norm[i], 2 * C_HEADS), jnp.tile(c_k_norm[i], 2 * C_HEADS)])
            qk = _qk_prep(lay, p, O_QK_WIDTH, gains.reshape(1, O_QK_WIDTH), cos, sin)
            lamv = jnp.stack([c_lambda_q1[i], c_lambda_k1[i], c_lambda_q2[i], c_lambda_k2[i]])
            lamv = jnp.pad(lamv, ((0, 4), (0, LANE - HEAD_DIM)))
            o_ctx = _diff_attn(qk, p, lamv, c_out_norm[i], lam_init, bc, tc, 0, tc)
            o_lat = _diff_attn(qk, p, lamv, c_out_norm[i], lam_init, bl, tl, lay.nctx, DIFF_Q_BLOCK,
                               cache=(cache_c_k, cache_c_v, i))
            o = jnp.concatenate([o_ctx, o_lat], axis=0)
            x = _linear_res(lay, [o], [o_w_out[i].astype(BF16)], mod, 2, x)
            ck_l.append(qk[:lay.nctx, 1024:2048].reshape(bc, tc, C_HEADS, 2, HEAD_DIM))
            cv_l.append(p[:lay.nctx, 2048:3072].reshape(bc, tc, C_HEADS, C_DV))
        x = _peer(lay, x, norm_ffn_g[l], mod, p_w_q[l].astype(BF16), _sub_key_blocks(p_sub_keys[l]),
                  p_u[l].astype(BF16), p_v[l].T.astype(BF16))

    return (x[:lay.nctx].reshape(bc, tc, d), x[lay.nctx:].reshape(bl, tl, d),
            jnp.stack(ak_l, axis=1), jnp.stack(av_l, axis=1), jnp.stack(sf_l, axis=1), jnp.stack(sb_l, axis=1),
            jnp.stack(ck_l, axis=1), jnp.stack(cv_l, axis=1))
```

```python
import functools
import math

import jax
import jax.numpy as jnp
from jax import lax
from jax.experimental import pallas as pl
from jax.experimental.pallas import tpu as pltpu

F32 = jnp.float32
BF16 = jnp.bfloat16
HI = lax.Precision.HIGHEST

D_MODEL = 1024
GRID_W = 64
HEAD_DIM = 64
ATTN_SCALE = HEAD_DIM ** -0.5
ROPE_BASE = 10000.0
RMS_EPS = 1e-6
N_MOD = 6
A_HEADS = 8
A_KV_HEADS = 2
A_WINDOW = 128
B_HEADS = 4
B_DK = 64
B_DV = 128
B_GATE_RANK = 16
B_GATE_TAU = 16.0
B_CHUNK = 64
C_HEADS = 8
C_DV = 128
P_HEADS = 8
P_NKEYS = 128
P_TOPK = 16

LANE = 128
SUBLANES = 8
ROW_BLOCK = 256
LIN_BLOCK = 512
PEER_TOKENS = 512
PEER_EXPERTS = SUBLANES * P_NKEYS
GATE_ROWS = 32
ATTN_Q_BLOCK = 128
DIFF_Q_BLOCK = 256
NEG_BIG = -1e30
VMEM_LIMIT = 56 * 1024 * 1024

E_AQ, E_AK, E_AV, E_BQ, E_BV, E_BR, E_BK, E_BG, E_WIDTH = 0, 512, 640, 768, 1024, 1536, 2048, 2304, 2432
E_QK_WIDTH = 640
O_QK_WIDTH = 2048


def _dot(a, b):
    return lax.dot_general(a.astype(BF16), b.astype(BF16), (((1,), (0,)), ((), ())),
                           preferred_element_type=F32)


def _dot_nt(a, b):
    return lax.dot_general(a.astype(BF16), b.astype(BF16), (((1,), (1,)), ((), ())),
                           preferred_element_type=F32)


def _dot_hi(a, b):
    return lax.dot_general(a, b, (((1,), (0,)), ((), ())), precision=HI, preferred_element_type=F32)


def _params(n_axes, flags=None):
    return pltpu.CompilerParams(dimension_semantics=("arbitrary",) * n_axes,
                                vmem_limit_bytes=VMEM_LIMIT, flags=flags)


def _lo_lanes(shape):
    return (lax.broadcasted_iota(jnp.int32, shape, len(shape) - 1) & 64) == 0


def _mod_body(c_ref, w_ref, b_ref, o_ref):
    c = c_ref[...]
    s = c * jax.nn.sigmoid(c)
    o_ref[0] = _dot_hi(s, w_ref[0]) + b_ref[0]


def _modulation(cvecs, ada_w, ada_b):
    depth, d, n = ada_w.shape
    r = cvecs.shape[0]
    tn = 1536
    return pl.pallas_call(
        _mod_body,
        grid=(depth, n // tn),
        in_specs=[pl.BlockSpec((r, d), lambda l, j: (0, 0)),
                  pl.BlockSpec((1, d, tn), lambda l, j: (l, 0, j)),
                  pl.BlockSpec((1, 1, tn), lambda l, j: (l, 0, j))],
        out_specs=pl.BlockSpec((1, r, tn), lambda l, j: (l, 0, j)),
        out_shape=jax.ShapeDtypeStruct((depth, r, n), F32),
        compiler_params=_params(2),
        name="modulation",
    )(cvecs, ada_w, ada_b.reshape(depth, 1, n))


class _Layout:
    def __init__(self, bc, tc, bl, tl):
        self.bc, self.tc, self.bl, self.tl = bc, tc, bl, tl
        self.nctx = bc * tc
        self.nt = self.nctx + bl * tl
        assert tc == ROW_BLOCK and tl % LIN_BLOCK == 0 and self.nctx % LIN_BLOCK == 0
        assert self.nctx % tl == 0 and tl % GRID_W == 0 and self.nt % PEER_TOKENS == 0
        assert self.nctx % PEER_TOKENS == 0 and tl % PEER_TOKENS == 0

    def mod_row(self, i, block):
        nctx_blocks = self.nctx // block
        return jnp.where(i < nctx_blocks, self.bl, (i - nctx_blocks) // (self.tl // block))


def _mod_spec(lay, block, chunk):
    return pl.BlockSpec((1, 1, D_MODEL), lambda i, *_: (lay.mod_row(i, block), 0, chunk))


def _rms_mod(x, g, sc, sh):
    ms = jnp.mean(x * x, axis=-1, keepdims=True)
    return (x * lax.rsqrt(ms + RMS_EPS) * g) * (1.0 + sc) + sh


def _ln_proj_body(x_ref, g_ref, sc_ref, sh_ref, w_ref, o_ref):
    h = _rms_mod(x_ref[...], g_ref[...], sc_ref[0], sh_ref[0])
    o_ref[...] = _dot(h, w_ref[...])


def _ln_proj(lay, x, g, mod, chunk, w):
    n_out = w.shape[1]
    return pl.pallas_call(
        _ln_proj_body,
        grid=(lay.nt // LIN_BLOCK,),
        in_specs=[pl.BlockSpec((LIN_BLOCK, D_MODEL), lambda i: (i, 0)),
                  pl.BlockSpec((1, D_MODEL), lambda i: (0, 0)),
                  _mod_spec(lay, LIN_BLOCK, chunk + 1),
                  _mod_spec(lay, LIN_BLOCK, chunk),
                  pl.BlockSpec((D_MODEL, n_out), lambda i: (0, 0))],
        out_specs=pl.BlockSpec((LIN_BLOCK, n_out), lambda i: (i, 0)),
        out_shape=jax.ShapeDtypeStruct((lay.nt, n_out), F32),
        compiler_params=_params(1),
        name="ln_proj",
    )(x, g.reshape(1, D_MODEL), mod, mod, w)


def _prep_body(x_ref, g_ref, cos_ref, sin_ref, bd_ref, o_ref, *, n_chunks):
    c = cos_ref[...]
    s = sin_ref[...]
    first = (lax.broadcasted_iota(jnp.int32, c.shape, 1) & 16) == 0
    for j in range(n_chunks):
        cols = slice(j * LANE, (j + 1) * LANE)
        x = x_ref[:, cols]
        ms = _dot_hi(x * x, bd_ref[...])
        y = x * lax.rsqrt(ms + RMS_EPS) * g_ref[:, cols]
        partner = jnp.where(first, pltpu.roll(y, LANE - 16, 1), pltpu.roll(y, 16, 1))
        o_ref[:, cols] = y * c + partner * s


def _rope_tables(tl):
    t = jnp.arange(tl)
    row = (t // GRID_W).astype(F32)
    col = (t % GRID_W).astype(F32)
    nf = HEAD_DIM // 4
    freqs = ROPE_BASE ** (-jnp.arange(nf, dtype=F32) / nf)
    ang_r = row[:, None] * freqs
    ang_c = col[:, None] * freqs
    cos64 = jnp.concatenate([jnp.cos(ang_r), jnp.cos(ang_r), jnp.cos(ang_c), jnp.cos(ang_c)], axis=1)
    sin64 = jnp.concatenate([-jnp.sin(ang_r), jnp.sin(ang_r), -jnp.sin(ang_c), jnp.sin(ang_c)], axis=1)
    cos = jnp.concatenate([jnp.ones((ROW_BLOCK, HEAD_DIM), F32), cos64], axis=0)
    sin = jnp.concatenate([jnp.zeros((ROW_BLOCK, HEAD_DIM), F32), sin64], axis=0)
    return jnp.tile(cos, (1, 2)), jnp.tile(sin, (1, 2))


def _qk_prep(lay, p, width, gains, cos, sin):
    n_ctx_blocks = lay.nctx // ROW_BLOCK
    per_seq = lay.tl // ROW_BLOCK

    def tab(i):
        return (jnp.where(i < n_ctx_blocks, 0, 1 + (i - n_ctx_blocks) % per_seq), 0)

    seg = jnp.arange(LANE) // HEAD_DIM
    bd = (seg[:, None] == seg[None, :]).astype(F32) / HEAD_DIM
    return pl.pallas_call(
        functools.partial(_prep_body, n_chunks=width // LANE),
        grid=(lay.nt // ROW_BLOCK,),
        in_specs=[pl.BlockSpec((ROW_BLOCK, width), lambda i: (i, 0)),
                  pl.BlockSpec((1, width), lambda i: (0, 0)),
                  pl.BlockSpec((ROW_BLOCK, LANE), tab),
                  pl.BlockSpec((ROW_BLOCK, LANE), tab),
                  pl.BlockSpec((LANE, LANE), lambda i: (0, 0))],
        out_specs=pl.BlockSpec((ROW_BLOCK, width), lambda i: (i, 0)),
        out_shape=jax.ShapeDtypeStruct((lay.nt, width), F32),
        compiler_params=_params(1),
        name="qk_prep",
    )(p, gains, cos, sin, bd)


def _dup_kv(x):
    lo = _lo_lanes(x.shape)
    xr = pltpu.roll(x, HEAD_DIM, 1)
    return jnp.where(lo, x, xr), jnp.where(lo, xr, x)


def _gqa_heads(sink_ref, q_ref, k2, v2, bias, o_ref):
    rows = q_ref.shape[0]
    lo = _lo_lanes((rows, LANE))
    for pair in range(A_HEADS // 2):
        kv = pair // 2
        cols = slice(pair * LANE, (pair + 1) * LANE)
        qp = q_ref[:, cols] * ATTN_SCALE
        outs = []
        for half in range(2):
            qm = jnp.where(lo, qp, 0.0) if half == 0 else jnp.where(lo, 0.0, qp)
            s = _dot_nt(qm, k2[kv])
            if bias is not None:
                s = s + bias
            sk = sink_ref[2 * pair + half]
            mx = jnp.maximum(jnp.max(s, axis=-1, keepdims=True), sk)
            p = jnp.exp(s - mx)
            den = jnp.sum(p, axis=-1, keepdims=True) + jnp.exp(sk - mx)
            outs.append(_dot(p, v2[kv]) / den)
        o_ref[:, cols] = jnp.where(lo, outs[0], outs[1])


def _attn_a_ctx_body(sink_ref, q_ref, k_ref, v_ref, o_ref):
    _gqa_heads(sink_ref, q_ref, _dup_kv(k_ref[...]), _dup_kv(v_ref[...]), None, o_ref)


def _attn_a_ctx(lay, qk, p, sink):
    return pl.pallas_call(
        _attn_a_ctx_body,
        grid=(lay.bc,),
        in_specs=[pl.BlockSpec(memory_space=pltpu.SMEM),
                  pl.BlockSpec((lay.tc, 512), lambda b: (b, 0)),
                  pl.BlockSpec((lay.tc, LANE), lambda b: (b, E_AK // LANE)),
                  pl.BlockSpec((lay.tc, LANE), lambda b: (b, E_AV // LANE))],
        out_specs=pl.BlockSpec((lay.tc, 512), lambda b: (b, 0)),
        out_shape=jax.ShapeDtypeStruct((lay.nctx, 512), F32),
        compiler_params=_params(1),
        name="attn_a_ctx",
    )(sink, qk, qk, p)


def _attn_a_lat_body(sink_ref, q_ref, k0, k1, k2r, v0, v1, v2r, kc_ref, vc_ref, o_ref):
    n = pl.program_id(1)
    last = pl.num_programs(1) - 1
    w = ATTN_Q_BLOCK
    lc = kc_ref.shape[0]
    k_all = jnp.concatenate([k0[...], k1[...], k2r[...], kc_ref[...]], axis=0)
    v_all = jnp.concatenate([v0[...], v1[...], v2r[...], vc_ref[...]], axis=0)
    i = lax.broadcasted_iota(jnp.int32, (w, 3 * w + lc), 0)
    j = lax.broadcasted_iota(jnp.int32, (w, 3 * w + lc), 1)
    valid = (jnp.abs(i + w - j) <= A_WINDOW) & ((j >= w) | (n > 0)) & ((j < 2 * w) | (n < last))
    bias = jnp.where(valid | (j >= 3 * w), 0.0, NEG_BIG)
    _gqa_heads(sink_ref, q_ref, _dup_kv(k_all), _dup_kv(v_all), bias, o_ref)


def _attn_a_lat(lay, qk, p, sink, cache_k, cache_v, layer):
    w = ATTN_Q_BLOCK
    nb = lay.tl // w
    base = lay.nctx // w
    lc = cache_k.shape[2]

    def row(off):
        return lambda b, n: base + b * nb + jnp.clip(n + off, 0, nb - 1)

    def kspec(off):
        r = row(off)
        return pl.BlockSpec((w, LANE), lambda b, n: (r(b, n), E_AK // LANE))

    def vspec(off):
        r = row(off)
        return pl.BlockSpec((w, LANE), lambda b, n: (r(b, n), E_AV // LANE))

    cspec = pl.BlockSpec((None, None, lc, LANE), lambda b, n: (b, layer, 0, 0))
    r0 = row(0)
    return pl.pallas_call(
        _attn_a_lat_body,
        grid=(lay.bl, nb),
        in_specs=[pl.BlockSpec(memory_space=pltpu.SMEM),
                  pl.BlockSpec((w, 512), lambda b, n: (r0(b, n), 0)),
                  kspec(-1), kspec(0), kspec(1), vspec(-1), vspec(0), vspec(1), cspec, cspec],
        out_specs=pl.BlockSpec((w, 512), lambda b, n: (b * nb + n, 0)),
        out_shape=jax.ShapeDtypeStruct((lay.bl * lay.tl, 512), F32),
        compiler_params=_params(2),
        name="attn_a_lat",
    )(sink, qk, qk, qk, qk, p, p, p,
      cache_k.reshape(cache_k.shape[:3] + (LANE,)), cache_v.reshape(cache_v.shape[:3] + (LANE,)))


def _gla_body(*refs, reverse, n_ctx_blocks, per_seq):
    if reverse:
        (bq_ref, bk_ref, bv_ref, bg_ref, gw_ref, gb_ref, s0_ref, tri_ref, of_ref, br_ref, bon_ref,
         o_ref, sfin_ref, st) = refs
    else:
        bq_ref, bk_ref, bv_ref, bg_ref, gw_ref, gb_ref, s0_ref, tri_ref, o_ref, sfin_ref, st = refs
    i = pl.program_id(0)
    is_start = (i < n_ctx_blocks) | (((i - n_ctx_blocks) % per_seq) == 0)

    @pl.when(is_start)
    def _():
        st[...] = s0_ref[...]

    z = _dot_hi(bg_ref[...], gw_ref[...]) + gb_ref[...]
    lg = (jnp.minimum(z, 0.0) - jnp.log1p(jnp.exp(-jnp.abs(z)))) * (1.0 / B_GATE_TAU)
    q = bq_ref[...] * (B_DK ** -0.5)
    k = bk_ref[...]
    tri = tri_ref[...]
    L = B_CHUNK
    n_chunks = ROW_BLOCK // L
    lo = _lo_lanes((L, LANE))
    ri = lax.broadcasted_iota(jnp.int32, (L, L), 0)
    ci = lax.broadcasted_iota(jnp.int32, (L, L), 1)
    causal = (ri <= ci) if reverse else (ri >= ci)
    order = range(n_chunks - 1, -1, -1) if reverse else range(n_chunks)
    for c in order:
        rows = slice(c * L, (c + 1) * L)
        b = _dot_hi(tri, lg[rows])
        b_last = b[0:1] if reverse else b[L - 1:L]
        qd = q[rows] * jnp.exp(b)
        kd = k[rows] * jnp.exp(-b)
        kt = k[rows] * jnp.exp(b_last - b)
        dec = jnp.exp(b_last)
        for h in range(B_HEADS):
            cols = slice((h // 2) * LANE, (h // 2 + 1) * LANE)
            mine = lo if h % 2 == 0 else jnp.logical_not(lo)
            qm = jnp.where(mine, qd[:, cols], 0.0)
            a = jnp.where(causal, _dot_nt(qm, kd[:, cols]), 0.0)
            vh = bv_ref[rows, h * B_DV:(h + 1) * B_DV]
            s_t = st[h]
            o = _dot(a, vh) + _dot_nt(qm, s_t)
            kv_t = _dot(vh.T, jnp.where(mine, kt[:, cols], 0.0))
            st[h] = s_t * dec[:, cols] + kv_t
            if reverse:
                tot = of_ref[rows, h * B_DV:(h + 1) * B_DV] + o
                ms = jnp.mean(tot * tot, axis=-1, keepdims=True)
                brh = br_ref[rows, h * B_DV:(h + 1) * B_DV]
                o = tot * lax.rsqrt(ms + RMS_EPS) * bon_ref[...] * (brh * jax.nn.sigmoid(brh))
            o_ref[rows, h * B_DV:(h + 1) * B_DV] = o
    sfin_ref[...] = st[...]


def _gla(lay, p, gw_pad, gb, s0, reverse, o_fwd=None, b_on=None):
    n_ctx_blocks = lay.nctx // ROW_BLOCK
    per_seq = lay.tl // ROW_BLOCK
    n_blocks = lay.nt // ROW_BLOCK

    def blk(i):
        j = i - n_ctx_blocks
        within = j % per_seq
        lat = n_ctx_blocks + (j - within) + (per_seq - 1 - within if reverse else within)
        return jnp.where(i < n_ctx_blocks, i, lat)

    def seq(i):
        return jnp.where(i < n_ctx_blocks, i, n_ctx_blocks + (i - n_ctx_blocks) // per_seq)

    def s0_idx(i):
        return jnp.where(i < n_ctx_blocks, lay.bl, (i - n_ctx_blocks) // per_seq)

    ri = jnp.arange(B_CHUNK)
    tri = ((ri[:, None] <= ri[None, :]) if reverse else (ri[:, None] >= ri[None, :])).astype(F32)
    in_specs = [pl.BlockSpec((ROW_BLOCK, 256), lambda i: (blk(i), E_BQ // 256)),
                pl.BlockSpec((ROW_BLOCK, 256), lambda i: (blk(i), E_BK // 256)),
                pl.BlockSpec((ROW_BLOCK, 512), lambda i: (blk(i), E_BV // 512)),
                pl.BlockSpec((ROW_BLOCK, LANE), lambda i: (blk(i), E_BG // LANE)),
                pl.BlockSpec((LANE, 256), lambda i: (0, 0)),
                pl.BlockSpec((1, 256), lambda i: (0, 0)),
                pl.BlockSpec((None, B_HEADS, LANE, LANE), lambda i: (s0_idx(i), 0, 0, 0)),
                pl.BlockSpec((B_CHUNK, B_CHUNK), lambda i: (0, 0))]
    args = [p, p, p, p, gw_pad, gb.reshape(1, 256), s0, tri]
    if reverse:
        in_specs += [pl.BlockSpec((ROW_BLOCK, 512), lambda i: (blk(i), 0)),
                     pl.BlockSpec((ROW_BLOCK, 512), lambda i: (blk(i), E_BR // 512)),
                     pl.BlockSpec((1, B_DV), lambda i: (0, 0))]
        args += [o_fwd, p, b_on.reshape(1, B_DV)]
    n_seq = n_ctx_blocks + lay.bl
    return pl.pallas_call(
        functools.partial(_gla_body, reverse=reverse, n_ctx_blocks=n_ctx_blocks, per_seq=per_seq),
        grid=(n_blocks,),
        in_specs=in_specs,
        out_specs=[pl.BlockSpec((ROW_BLOCK, 512), lambda i: (blk(i), 0)),
                   pl.BlockSpec((None, B_HEADS, LANE, LANE), lambda i: (seq(i), 0, 0, 0))],
        out_shape=[jax.ShapeDtypeStruct((lay.nt, 512), F32),
                   jax.ShapeDtypeStruct((n_seq, B_HEADS, LANE, LANE), F32)],
        scratch_shapes=[pltpu.VMEM((B_HEADS, LANE, LANE), F32)],
        compiler_params=_params(1),
        name="gla_bwd" if reverse else "gla_fwd",
    )(*args)


def _state_in(s):
    st = jnp.swapaxes(s.astype(F32), 2, 3)
    z = jnp.zeros_like(st)
    even = jnp.concatenate([st, z], axis=-1)
    odd = jnp.concatenate([z, st], axis=-1)
    pick = (jnp.arange(B_HEADS) % 2 == 0)[None, :, None, None]
    full = jnp.where(pick, even, odd)
    return jnp.concatenate([full, jnp.zeros_like(full[:1])], axis=0)


def _state_out(s_t):
    even = s_t[..., :B_DK]
    odd = s_t[..., B_DK:]
    pick = (jnp.arange(B_HEADS) % 2 == 0)[None, :, None, None]
    return jnp.swapaxes(jnp.where(pick, even, odd), 2, 3)


def _linear_res_body(*refs, n_in):
    a_refs, w_refs = refs[:n_in], refs[n_in:2 * n_in]
    gate_ref, res_ref, o_ref = refs[2 * n_in:]
    acc = _dot(a_refs[0][...], w_refs[0][...])
    for a_ref, w_ref in zip(a_refs[1:], w_refs[1:]):
        acc = acc + _dot(a_ref[...], w_ref[...])
    o_ref[...] = res_ref[...] + gate_ref[0] * acc


def _linear_res(lay, xs, ws, mod, chunk, res):
    n_in = len(xs)
    in_specs = [pl.BlockSpec((LIN_BLOCK, a.shape[1]), lambda i: (i, 0)) for a in xs]
    in_specs += [pl.BlockSpec(w.shape, lambda i: (0, 0)) for w in ws]
    in_specs += [_mod_spec(lay, LIN_BLOCK, chunk),
                 pl.BlockSpec((LIN_BLOCK, D_MODEL), lambda i: (i, 0))]
    return pl.pallas_call(
        functools.partial(_linear_res_body, n_in=n_in),
        grid=(lay.nt // LIN_BLOCK,),
        in_specs=in_specs,
        out_specs=pl.BlockSpec((LIN_BLOCK, D_MODEL), lambda i: (i, 0)),
        out_shape=jax.ShapeDtypeStruct((lay.nt, D_MODEL), F32),
        compiler_params=_params(1),
        name="linear_res",
    )(*xs, *ws, mod, res)


def _diff_attn_body(*refs, lam_init, has_cache):
    if has_cache:
        lamv_ref, con_ref, q_ref, k_ref, v_ref, kc_ref, vc_ref, o_ref = refs
    else:
        lamv_ref, con_ref, q_ref, k_ref, v_ref, o_ref = refs
    lv = lamv_ref[...]
    lam = (jnp.exp(jnp.sum(lv[0:1] * lv[1:2], axis=-1, keepdims=True))
           - jnp.exp(jnp.sum(lv[2:3] * lv[3:4], axis=-1, keepdims=True)) + lam_init)
    q = q_ref[...] * ATTN_SCALE
    tq = q.shape[0]
    lo = _lo_lanes(q.shape)
    qs = jnp.concatenate([jnp.where(lo, q, 0.0), jnp.where(lo, 0.0, q)], axis=0)
    s = _dot_nt(qs, k_ref[...])
    mx = jnp.max(s, axis=-1, keepdims=True)
    if has_cache:
        sc = _dot_nt(qs, kc_ref[...])
        mx = jnp.maximum(mx, jnp.max(sc, axis=-1, keepdims=True))
    p = jnp.exp(s - mx)
    den = jnp.sum(p, axis=-1, keepdims=True)
    if has_cache:
        pc = jnp.exp(sc - mx)
        den = den + jnp.sum(pc, axis=-1, keepdims=True)
    inv = 1.0 / den
    w0 = inv[:tq]
    w1 = lam * inv[tq:]
    o = _dot(p[:tq] * w0 - p[tq:] * w1, v_ref[...])
    if has_cache:
        o = o + _dot(pc[:tq] * w0 - pc[tq:] * w1, vc_ref[...])
    ms = jnp.mean(o * o, axis=-1, keepdims=True)
    o_ref[...] = o * lax.rsqrt(ms + RMS_EPS) * con_ref[...] * (1.0 - lam_init)


def _diff_attn(qk, p, lamv, c_on, lam_init, n_seq, t_seq, row_base, tq, cache=None):
    nq = t_seq // tq
    qbase = row_base // tq
    kbase = row_base // t_seq
    in_specs = [pl.BlockSpec((8, LANE), lambda b, h, i: (0, 0)),
                pl.BlockSpec((1, LANE), lambda b, h, i: (0, 0)),
                pl.BlockSpec((tq, LANE), lambda b, h, i: (qbase + b * nq + i, h)),
                pl.BlockSpec((t_seq, LANE), lambda b, h, i: (kbase + b, C_HEADS + h)),
                pl.BlockSpec((t_seq, LANE), lambda b, h, i: (kbase + b, 2 * C_HEADS + h))]
    args = [lamv, c_on.reshape(1, C_DV), qk, qk, p]
    if cache is not None:
        ck, cv, layer = cache
        lc = ck.shape[2]
        in_specs += [pl.BlockSpec((None, None, lc, LANE), lambda b, h, i: (b, layer, 0, h)),
                     pl.BlockSpec((None, None, lc, LANE), lambda b, h, i: (b, layer, 0, h))]
        args += [ck.reshape(ck.shape[:3] + (C_HEADS * LANE,)), cv.reshape(cv.shape[:3] + (C_HEADS * LANE,))]
    return pl.pallas_call(
        functools.partial(_diff_attn_body, lam_init=lam_init, has_cache=cache is not None),
        grid=(n_seq, C_HEADS, nq),
        in_specs=in_specs,
        out_specs=pl.BlockSpec((tq, LANE), lambda b, h, i: (b * nq + i, h)),
        out_shape=jax.ShapeDtypeStruct((n_seq * t_seq, C_HEADS * C_DV), F32),
        compiler_params=_params(3),
        name="diff_attn",
    )(*args)


def _top_rows(s, k, out_ref, want_rank=False):
    cur = s
    first = None
    rank = jnp.full(s.shape, float(k), F32) if want_rank else None
    for r in range(k):
        m = jnp.max(cur, axis=0, keepdims=True)
        out_ref[r:r + 1, :] = m
        first = m if first is None else first
        hit = cur == m
        if want_rank:
            rank = jnp.where(hit, float(r), rank)
        cur = jnp.where(hit, NEG_BIG, cur)
    return first, m, rank


_CAND_COUNTS = tuple(P_TOPK // (a + 1) for a in range(P_TOPK))
_CAND_ROWS = SUBLANES * ((sum(_CAND_COUNTS) + SUBLANES - 1) // SUBLANES)


def _peer_select(x_ref, g_ref, sc_ref, sh_ref, wq_ref, sk_ref,
                 h_s, q_s, r0_s, e0_s, n1_s, e1_s, top0_s, top1_s, cand_s):
    h = _rms_mod(x_ref[...], g_ref[...], sc_ref[0], sh_ref[0])
    h_s[...] = h.T.astype(BF16)
    q_s[...] = _dot(h, wq_ref[...])

    def head(hd, carry):
        qp = q_s[:, pl.ds(pl.multiple_of(hd * LANE, LANE), LANE)]
        s_t = lax.dot_general(sk_ref[hd], qp, (((1,), (1,)), ((), ())), precision=HI,
                              preferred_element_type=F32)
        for t in range(s_t.shape[1] // LANE):
            cols = slice(t * LANE, (t + 1) * LANE)
            s0 = s_t[:P_NKEYS, cols]
            s1 = s_t[P_NKEYS:, cols]
            max0, _, r0 = _top_rows(s0, P_TOPK, top0_s, want_rank=True)
            max1, _, _ = _top_rows(s1, P_TOPK, top1_s)
            row = 0
            for a, cnt in enumerate(_CAND_COUNTS):
                cand_s[row:row + cnt, :] = top0_s[a:a + 1, :] + top1_s[0:cnt, :]
                row += cnt
            cand_s[row:, :] = jnp.full((_CAND_ROWS - row, LANE), NEG_BIG, F32)
            cand = cand_s[...]
            _, thr, _ = _top_rows(cand, P_TOPK, top1_s)
            mx = max0 + max1
            z = jnp.sum(jnp.where(cand >= thr, jnp.exp(cand - mx), 0.0), axis=0, keepdims=True)
            n1 = jnp.zeros(s1.shape, F32)
            for a in range(P_TOPK // 2):
                n1 = n1 + jnp.where((top0_s[a:a + 1, :] + s1) >= thr, 1.0, 0.0)
            n_hi = jnp.zeros(max1.shape, F32)
            for a in range(P_TOPK // 2, P_TOPK):
                n_hi = n_hi + jnp.where((top0_s[a:a + 1, :] + max1) >= thr, 1.0, 0.0)
            n1 = n1 + jnp.where(s1 == max1, n_hi, 0.0)
            r0_s[hd, :, cols] = r0
            e0_s[hd, :, cols] = jnp.exp(s0 - max0)
            n1_s[hd, t] = n1.astype(BF16)
            e1_s[hd, t] = (jnp.exp(s1 - max1) / z).astype(BF16)
        return carry

    lax.fori_loop(0, P_HEADS, head, 0)


def _bf16_pair_words(x):
    bits = lax.bitcast_convert_type(x.astype(BF16).astype(F32), jnp.uint32)
    return bits | (bits >> 16)


def _packed_row(w):
    return pltpu.bitcast(jnp.broadcast_to(w, (GATE_ROWS // 2, LANE)), BF16)


def _peer_step(group, u_ref, vt_ref, h_s, r0_s, e0_s, r0c_s, e0c_s, n1_s, e1_s,
               st_score, st_gate, wt_gate, wt_acc, acc_s, *, do_acc, do_gate, do_score):
    tn = h_s.shape[1]
    n_t = tn // LANE
    kc = u_ref.shape[0] // n_t
    if do_gate:
        r0c_s[...] = _bf16_pair_words(r0_s[:, group, :])
        e0c_s[...] = _bf16_pair_words(e0_s[:, group, :])
    zero = jnp.zeros((GATE_ROWS, LANE), BF16)
    for t in range(n_t):
        cols = slice(t * LANE, (t + 1) * LANE)
        ks = slice(t * kc, (t + 1) * kc)
        for jq in range(P_NKEYS // GATE_ROWS):
            jr = slice(jq * GATE_ROWS, (jq + 1) * GATE_ROWS)
            g = [zero for _ in range(SUBLANES)]
            for hd in range(P_HEADS if do_gate else 0):
                n1q = n1_s[hd, t, jr, :]
                e1q = e1_s[hd, t, jr, :]
                for il in range(SUBLANES):
                    r0r = _packed_row(r0c_s[hd, il:il + 1, cols])
                    e0r = _packed_row(e0c_s[hd, il:il + 1, cols])
                    g[il] = g[il] + e0r * jnp.where(r0r < n1q, e1q, zero)
            for il in range(SUBLANES if do_gate else 0):
                rows = slice(il * P_NKEYS + jq * GATE_ROWS, il * P_NKEYS + (jq + 1) * GATE_ROWS)
                sc = st_gate[t, rows, :]
                act = 0.5 * sc * (1.0 + lax.erf(sc * (2.0 ** -0.5)))
                wt_gate[rows, cols] = (g[il].astype(F32) * act).astype(BF16)
            if jq == 1 and do_acc:
                acc_s[...] += _dot(vt_ref[:, ks], wt_acc[ks, :])
            if jq == 3 and do_score:
                s_new = _dot(u_ref[ks, :], h_s[...])
                for tt in range(n_t):
                    st_score[tt, ks, :] = s_new[:, tt * LANE:(tt + 1) * LANE]


def _peer_body(x_ref, g_ref, sc_ref, sh_ref, gate_ref, wq_ref, sk_ref, u_ref, vt_ref, o_ref,
               h_s, q_s, r0_s, e0_s, r0c_s, e0c_s, n1_s, e1_s, top0_s, top1_s, cand_s,
               st0_s, st1_s, wt0_s, wt1_s, acc_s, *, n_eb):
    e = pl.program_id(1)
    assert u_ref.shape[0] == SUBLANES * P_NKEYS
    assert n_eb >= 2

    @pl.when(e == 0)
    def _():
        _peer_select(x_ref, g_ref, sc_ref, sh_ref, wq_ref, sk_ref,
                     h_s, q_s, r0_s, e0_s, n1_s, e1_s, top0_s, top1_s, cand_s)
        acc_s[...] = jnp.zeros_like(acc_s)

    gate_block = jnp.clip(e - 1, 0, n_eb - 1)
    group = pl.ds(pl.multiple_of(gate_block * SUBLANES, SUBLANES), SUBLANES)
    st = (st0_s, st1_s)
    wt = (wt0_s, wt1_s)

    def step(cond, slot, **parts):
        @pl.when(cond)
        def _():
            _peer_step(group, u_ref, vt_ref, h_s, r0_s, e0_s, r0c_s, e0c_s, n1_s, e1_s,
                       st[slot], st[1 - slot], wt[1 - slot], wt[slot], acc_s, **parts)

    step(e == 0, 0, do_acc=False, do_gate=False, do_score=True)
    step(e == 1, 1, do_acc=False, do_gate=True, do_score=True)
    for slot in range(2):
        step((e >= 2) & (e < n_eb) & (e % 2 == slot), slot, do_acc=True, do_gate=True, do_score=True)
    step(e == n_eb, n_eb % 2, do_acc=True, do_gate=True, do_score=False)
    step(e == n_eb + 1, (n_eb + 1) % 2, do_acc=True, do_gate=False, do_score=False)

    @pl.when(e == n_eb + 1)
    def _():
        o_ref[...] = x_ref[...] + gate_ref[0] * acc_s[...].T


def _peer(lay, x, g, mod, wq, sk2, u, vt):
    tn, en = PEER_TOKENS, PEER_EXPERTS
    n_eb = u.shape[0] // en
    grid = (lay.nt // tn, n_eb + 2)
    head_scr = pltpu.VMEM((P_HEADS, P_NKEYS, tn), F32)
    rows_scr = pltpu.VMEM((P_HEADS, SUBLANES, tn), jnp.uint32)
    tile_scr = pltpu.VMEM((P_HEADS, tn // LANE, P_NKEYS, LANE), BF16)
    score_scr = pltpu.VMEM((tn // LANE, en, LANE), F32)
    gated_scr = pltpu.VMEM((en, tn), BF16)
    return pl.pallas_call(
        functools.partial(_peer_body, n_eb=n_eb),
        grid=grid,
        in_specs=[pl.BlockSpec((tn, D_MODEL), lambda i, e: (i, 0)),
                  pl.BlockSpec((1, D_MODEL), lambda i, e: (0, 0)),
                  _mod_spec(lay, tn, 4), _mod_spec(lay, tn, 3), _mod_spec(lay, tn, 5),
                  pl.BlockSpec((D_MODEL, P_HEADS * LANE), lambda i, e: (0, 0)),
                  pl.BlockSpec((P_HEADS, 2 * P_NKEYS, LANE), lambda i, e: (0, 0, 0)),
                  pl.BlockSpec((en, D_MODEL), lambda i, e: (jnp.minimum(e, n_eb - 1), 0)),
                  pl.BlockSpec((D_MODEL, en), lambda i, e: (0, jnp.clip(e - 2, 0, n_eb - 1)))],
        out_specs=pl.BlockSpec((tn, D_MODEL), lambda i, e: (i, 0)),
        out_shape=jax.ShapeDtypeStruct((lay.nt, D_MODEL), F32),
        scratch_shapes=[pltpu.VMEM((D_MODEL, tn), BF16),
                        pltpu.VMEM((tn, P_HEADS * LANE), F32),
                        head_scr, head_scr, rows_scr, rows_scr, tile_scr, tile_scr,
                        pltpu.VMEM((P_TOPK, LANE), F32),
                        pltpu.VMEM((P_TOPK, LANE), F32),
                        pltpu.VMEM((_CAND_ROWS, LANE), F32),
                        score_scr, score_scr, gated_scr, gated_scr,
                        pltpu.VMEM((D_MODEL, tn), F32)],
        compiler_params=_params(2),
        name="peer",
    )(x, g.reshape(1, D_MODEL), mod, mod, mod, wq, sk2, u, vt)


def _sub_key_blocks(sub_keys):
    z = jnp.zeros_like(sub_keys[:, 0])
    top = jnp.concatenate([sub_keys[:, 0], z], axis=-1)
    bot = jnp.concatenate([z, sub_keys[:, 1]], axis=-1)
    return jnp.concatenate([top, bot], axis=1)


def _lambda_init(layer):
    return 0.8 - 0.6 * math.exp(-0.3 * layer)


def _even_w_in(w):
    aq, ak, av, bq, bk, bv, br, bg = jnp.split(w, [512, 640, 768, 1024, 1280, 1792, 2304], axis=1)
    pad = jnp.zeros((w.shape[0], E_WIDTH - E_BG - bg.shape[1]), w.dtype)
    return jnp.concatenate([aq, ak, av, bq, bv, br, bk, bg, pad], axis=1).astype(BF16)


def kernel(x_prompt, x_sample, cache_a_k, cache_a_v, state_b_fwd, state_b_bwd, cache_c_k, cache_c_v, c, c_ctx, ada_w, ada_b, norm_mix_g, norm_ffn_g, e_w_in, e_w_out, a_q_norm, a_k_norm, a_sink, b_gate_w_f, b_gate_b_f, b_gate_w_b, b_gate_b_b, b_out_norm, o_w_in, o_w_out, c_q_norm, c_k_norm, c_lambda_q1, c_lambda_k1, c_lambda_q2, c_lambda_k2, c_out_norm, p_w_q, p_sub_keys, p_u, p_v):
    bc, tc, d = x_prompt.shape
    bl, tl, _ = x_sample.shape
    depth = ada_w.shape[0]
    lay = _Layout(bc, tc, bl, tl)
    n_mod_rows = 8 * ((bl + 1 + 7) // 8)
    cvecs = jnp.concatenate([c, c_ctx[None, :], jnp.zeros((n_mod_rows - bl - 1, d), F32)], axis=0)
    mods = _modulation(cvecs, ada_w, ada_b)
    cos, sin = _rope_tables(tl)
    x = jnp.concatenate([x_prompt.reshape(lay.nctx, d), x_sample.reshape(bl * tl, d)], axis=0)

    ak_l, av_l, sf_l, sb_l, ck_l, cv_l = [], [], [], [], [], []
    for l in range(depth):
        i = l // 2
        mod = mods[l].reshape(n_mod_rows, 1, N_MOD * d)
        if l % 2 == 0:
            p = _ln_proj(lay, x, norm_mix_g[l], mod, 0, _even_w_in(e_w_in[i]))
            gains = jnp.concatenate([jnp.tile(a_q_norm[i], A_HEADS), jnp.tile(a_k_norm[i], A_KV_HEADS)])
            qk = _qk_prep(lay, p, E_QK_WIDTH, gains.reshape(1, E_QK_WIDTH), cos, sin)
            sink = a_sink[i].reshape(A_HEADS)
            oa = jnp.concatenate([_attn_a_ctx(lay, qk, p, sink),
                                  _attn_a_lat(lay, qk, p, sink, cache_a_k, cache_a_v, i)], axis=0)
            zrow = jnp.zeros((LANE - 2 * B_GATE_RANK, 256), F32)
            zgate = jnp.zeros((B_GATE_RANK, 256), F32)
            gw_f = jnp.concatenate([b_gate_w_f[i], zgate, zrow], axis=0)
            gw_b = jnp.concatenate([zgate, b_gate_w_b[i], zrow], axis=0)
            o_f, s_f = _gla(lay, p, gw_f, b_gate_b_f[i], _state_in(state_b_fwd[:, i]), False)
            ob, s_b = _gla(lay, p, gw_b, b_gate_b_b[i], _state_in(state_b_bwd[:, i]), True,
                           o_fwd=o_f, b_on=b_out_norm[i])
            w_out = e_w_out[i].astype(BF16)
            x = _linear_res(lay, [oa, ob], [w_out[:512], w_out[512:]], mod, 2, x)
            ak_l.append(qk[:lay.nctx, E_AK:E_AK + LANE].reshape(bc, tc, A_KV_HEADS, HEAD_DIM))
            av_l.append(p[:lay.nctx, E_AV:E_AV + LANE].reshape(bc, tc, A_KV_HEADS, HEAD_DIM))
            sf_l.append(_state_out(s_f[:bc]))
            sb_l.append(_state_out(s_b[:bc]))
        else:
            lam_init = _lambda_init(l)
            p = _ln_proj(lay, x, norm_mix_g[l], mod, 0, o_w_in[i].astype(BF16))
            gains = jnp.concatenate([jnp.tile(c_q_norm[i], 2 * C_HEADS), jnp.tile(c_k_norm[i], 2 * C_HEADS)])
            qk = _qk_prep(lay, p, O_QK_WIDTH, gains.reshape(1, O_QK_WIDTH), cos, sin)
            lamv = jnp.stack([c_lambda_q1[i], c_lambda_k1[i], c_lambda_q2[i], c_lambda_k2[i]])
            lamv = jnp.pad(lamv, ((0, 4), (0, LANE - HEAD_DIM)))
            o_ctx = _diff_attn(qk, p, lamv, c_out_norm[i], lam_init, bc, tc, 0, tc)
            o_lat = _diff_attn(qk, p, lamv, c_out_norm[i], lam_init, bl, tl, lay.nctx, DIFF_Q_BLOCK,
                               cache=(cache_c_k, cache_c_v, i))
            o = jnp.concatenate([o_ctx, o_lat], axis=0)
            x = _linear_res(lay, [o], [o_w_out[i].astype(BF16)], mod, 2, x)
            ck_l.append(qk[:lay.nctx, 1024:2048].reshape(bc, tc, C_HEADS, 2, HEAD_DIM))
            cv_l.append(p[:lay.nctx, 2048:3072].reshape(bc, tc, C_HEADS, C_DV))
        x = _peer(lay, x, norm_ffn_g[l], mod, p_w_q[l].astype(BF16), _sub_key_blocks(p_sub_keys[l]),
                  p_u[l].astype(BF16), p_v[l].T.astype(BF16))

    return (x[:lay.nctx].reshape(bc, tc, d), x[lay.nctx:].reshape(bl, tl, d),
            jnp.stack(ak_l, axis=1), jnp.stack(av_l, axis=1), jnp.stack(sf_l, axis=1), jnp.stack(sb_l, axis=1),
            jnp.stack(ck_l, axis=1), jnp.stack(cv_l, axis=1))
```

```python
import functools
import math

import jax
import jax.numpy as jnp
from jax import lax
from jax.experimental import pallas as pl
from jax.experimental.pallas import tpu as pltpu

F32 = jnp.float32
BF16 = jnp.bfloat16
HI = lax.Precision.HIGHEST

D_MODEL = 1024
GRID_W = 64
HEAD_DIM = 64
ATTN_SCALE = HEAD_DIM ** -0.5
ROPE_BASE = 10000.0
RMS_EPS = 1e-6
N_MOD = 6
A_HEADS = 8
A_KV_HEADS = 2
A_WINDOW = 128
B_HEADS = 4
B_DK = 64
B_DV = 128
B_GATE_RANK = 16
B_GATE_TAU = 16.0
B_CHUNK = 64
C_HEADS = 8
C_DV = 128
P_HEADS = 8
P_NKEYS = 128
P_TOPK = 16

LANE = 128
SUBLANES = 8
ROW_BLOCK = 256
LIN_BLOCK = 512
PEER_TOKENS = 512
PEER_EXPERTS = SUBLANES * P_NKEYS
GATE_ROWS = 32
ATTN_Q_BLOCK = 128
DIFF_Q_BLOCK = 256
NEG_BIG = -1e30
VMEM_LIMIT = 56 * 1024 * 1024

E_AQ, E_AK, E_AV, E_BQ, E_BV, E_BR, E_BK, E_BG, E_WIDTH = 0, 512, 640, 768, 1024, 1536, 2048, 2304, 2432
E_QK_WIDTH = 640
O_QK_WIDTH = 2048


def _dot(a, b):
    return lax.dot_general(a.astype(BF16), b.astype(BF16), (((1,), (0,)), ((), ())),
                           preferred_element_type=F32)


def _dot_nt(a, b):
    return lax.dot_general(a.astype(BF16), b.astype(BF16), (((1,), (1,)), ((), ())),
                           preferred_element_type=F32)


def _dot_hi(a, b):
    return lax.dot_general(a, b, (((1,), (0,)), ((), ())), precision=HI, preferred_element_type=F32)


def _params(n_axes, flags=None):
    return pltpu.CompilerParams(dimension_semantics=("arbitrary",) * n_axes,
                                vmem_limit_bytes=VMEM_LIMIT, flags=flags)


def _lo_lanes(shape):
    return (lax.broadcasted_iota(jnp.int32, shape, len(shape) - 1) & 64) == 0


def _mod_body(c_ref, w_ref, b_ref, o_ref):
    c = c_ref[...]
    s = c * jax.nn.sigmoid(c)
    o_ref[0] = _dot_hi(s, w_ref[0]) + b_ref[0]


def _modulation(cvecs, ada_w, ada_b):
    depth, d, n = ada_w.shape
    r = cvecs.shape[0]
    tn = 1536
    return pl.pallas_call(
        _mod_body,
        grid=(depth, n // tn),
        in_specs=[pl.BlockSpec((r, d), lambda l, j: (0, 0)),
                  pl.BlockSpec((1, d, tn), lambda l, j: (l, 0, j)),
                  pl.BlockSpec((1, 1, tn), lambda l, j: (l, 0, j))],
        out_specs=pl.BlockSpec((1, r, tn), lambda l, j: (l, 0, j)),
        out_shape=jax.ShapeDtypeStruct((depth, r, n), F32),
        compiler_params=_params(2),
        name="modulation",
    )(cvecs, ada_w, ada_b.reshape(depth, 1, n))


class _Layout:
    def __init__(self, bc, tc, bl, tl):
        self.bc, self.tc, self.bl, self.tl = bc, tc, bl, tl
        self.nctx = bc * tc
        self.nt = self.nctx + bl * tl
        assert tc == ROW_BLOCK and tl % LIN_BLOCK == 0 and self.nctx % LIN_BLOCK == 0
        assert self.nctx % tl == 0 and tl % GRID_W == 0 and self.nt % PEER_TOKENS == 0
        assert self.nctx % PEER_TOKENS == 0 and tl % PEER_TOKENS == 0

    def mod_row(self, i, block):
        nctx_blocks = self.nctx // block
        return jnp.where(i < nctx_blocks, self.bl, (i - nctx_blocks) // (self.tl // block))


def _mod_spec(lay, block, chunk):
    return pl.BlockSpec((1, 1, D_MODEL), lambda i, *_: (lay.mod_row(i, block), 0, chunk))


def _rms_mod(x, g, sc, sh):
    ms = jnp.mean(x * x, axis=-1, keepdims=True)
    return (x * lax.rsqrt(ms + RMS_EPS) * g) * (1.0 + sc) + sh


def _ln_proj_body(x_ref, g_ref, sc_ref, sh_ref, w_ref, o_ref):
    h = _rms_mod(x_ref[...], g_ref[...], sc_ref[0], sh_ref[0])
    o_ref[...] = _dot(h, w_ref[...])


def _ln_proj(lay, x, g, mod, chunk, w):
    n_out = w.shape[1]
    return pl.pallas_call(
        _ln_proj_body,
        grid=(lay.nt // LIN_BLOCK,),
        in_specs=[pl.BlockSpec((LIN_BLOCK, D_MODEL), lambda i: (i, 0)),
                  pl.BlockSpec((1, D_MODEL), lambda i: (0, 0)),
                  _mod_spec(lay, LIN_BLOCK, chunk + 1),
                  _mod_spec(lay, LIN_BLOCK, chunk),
                  pl.BlockSpec((D_MODEL, n_out), lambda i: (0, 0))],
        out_specs=pl.BlockSpec((LIN_BLOCK, n_out), lambda i: (i, 0)),
        out_shape=jax.ShapeDtypeStruct((lay.nt, n_out), F32),
        compiler_params=_params(1),
        name="ln_proj",
    )(x, g.reshape(1, D_MODEL), mod, mod, w)


def _prep_body(x_ref, g_ref, cos_ref, sin_ref, bd_ref, o_ref, *, n_chunks):
    c = cos_ref[...]
    s = sin_ref[...]
    first = (lax.broadcasted_iota(jnp.int32, c.shape, 1) & 16) == 0
    for j in range(n_chunks):
        cols = slice(j * LANE, (j + 1) * LANE)
        x = x_ref[:, cols]
        ms = _dot_hi(x * x, bd_ref[...])
        y = x * lax.rsqrt(ms + RMS_EPS) * g_ref[:, cols]
        partner = jnp.where(first, pltpu.roll(y, LANE - 16, 1), pltpu.roll(y, 16, 1))
        o_ref[:, cols] = y * c + partner * s


def _rope_tables(tl):
    t = jnp.arange(tl)
    row = (t // GRID_W).astype(F32)
    col = (t % GRID_W).astype(F32)
    nf = HEAD_DIM // 4
    freqs = ROPE_BASE ** (-jnp.arange(nf, dtype=F32) / nf)
    ang_r = row[:, None] * freqs
    ang_c = col[:, None] * freqs
    cos64 = jnp.concatenate([jnp.cos(ang_r), jnp.cos(ang_r), jnp.cos(ang_c), jnp.cos(ang_c)], axis=1)
    sin64 = jnp.concatenate([-jnp.sin(ang_r), jnp.sin(ang_r), -jnp.sin(ang_c), jnp.sin(ang_c)], axis=1)
    cos = jnp.concatenate([jnp.ones((ROW_BLOCK, HEAD_DIM), F32), cos64], axis=0)
    sin = jnp.concatenate([jnp.zeros((ROW_BLOCK, HEAD_DIM), F32), sin64], axis=0)
    return jnp.tile(cos, (1, 2)), jnp.tile(sin, (1, 2))


def _qk_prep(lay, p, width, gains, cos, sin):
    n_ctx_blocks = lay.nctx // ROW_BLOCK
    per_seq = lay.tl // ROW_BLOCK

    def tab(i):
        return (jnp.where(i < n_ctx_blocks, 0, 1 + (i - n_ctx_blocks) % per_seq), 0)

    seg = jnp.arange(LANE) // HEAD_DIM
    bd = (seg[:, None] == seg[None, :]).astype(F32) / HEAD_DIM
    return pl.pallas_call(
        functools.partial(_prep_body, n_chunks=width // LANE),
        grid=(lay.nt // ROW_BLOCK,),
        in_specs=[pl.BlockSpec((ROW_BLOCK, width), lambda i: (i, 0)),
                  pl.BlockSpec((1, width), lambda i: (0, 0)),
                  pl.BlockSpec((ROW_BLOCK, LANE), tab),
                  pl.BlockSpec((ROW_BLOCK, LANE), tab),
                  pl.BlockSpec((LANE, LANE), lambda i: (0, 0))],
        out_specs=pl.BlockSpec((ROW_BLOCK, width), lambda i: (i, 0)),
        out_shape=jax.ShapeDtypeStruct((lay.nt, width), F32),
        compiler_params=_params(1),
        name="qk_prep",
    )(p, gains, cos, sin, bd)


def _dup_kv(x):
    lo = _lo_lanes(x.shape)
    xr = pltpu.roll(x, HEAD_DIM, 1)
    return jnp.where(lo, x, xr), jnp.where(lo, xr, x)


def _gqa_heads(sink_ref, q_ref, k2, v2, bias, o_ref):
    rows = q_ref.shape[0]
    lo = _lo_lanes((rows, LANE))
    for pair in range(A_HEADS // 2):
        kv = pair // 2
        cols = slice(pair * LANE, (pair + 1) * LANE)
        qp = q_ref[:, cols] * ATTN_SCALE
        outs = []
        for half in range(2):
            qm = jnp.where(lo, qp, 0.0) if half == 0 else jnp.where(lo, 0.0, qp)
            s = _dot_nt(qm, k2[kv])
            if bias is not None:
                s = s + bias
            sk = sink_ref[2 * pair + half]
            mx = jnp.maximum(jnp.max(s, axis=-1, keepdims=True), sk)
            p = jnp.exp(s - mx)
            den = jnp.sum(p, axis=-1, keepdims=True) + jnp.exp(sk - mx)
            outs.append(_dot(p, v2[kv]) / den)
        o_ref[:, cols] = jnp.where(lo, outs[0], outs[1])


def _attn_a_ctx_body(sink_ref, q_ref, k_ref, v_ref, o_ref):
    _gqa_heads(sink_ref, q_ref, _dup_kv(k_ref[...]), _dup_kv(v_ref[...]), None, o_ref)


def _attn_a_ctx(lay, qk, p, sink):
    return pl.pallas_call(
        _attn_a_ctx_body,
        grid=(lay.bc,),
        in_specs=[pl.BlockSpec(memory_space=pltpu.SMEM),
                  pl.BlockSpec((lay.tc, 512), lambda b: (b, 0)),
                  pl.BlockSpec((lay.tc, LANE), lambda b: (b, E_AK // LANE)),
                  pl.BlockSpec((lay.tc, LANE), lambda b: (b, E_AV // LANE))],
        out_specs=pl.BlockSpec((lay.tc, 512), lambda b: (b, 0)),
        out_shape=jax.ShapeDtypeStruct((lay.nctx, 512), F32),
        compiler_params=_params(1),
        name="attn_a_ctx",
    )(sink, qk, qk, p)


def _attn_a_lat_body(sink_ref, q_ref, k0, k1, k2r, v0, v1, v2r, kc_ref, vc_ref, o_ref):
    n = pl.program_id(1)
    last = pl.num_programs(1) - 1
    w = ATTN_Q_BLOCK
    lc = kc_ref.shape[0]
    k_all = jnp.concatenate([k0[...], k1[...], k2r[...], kc_ref[...]], axis=0)
    v_all = jnp.concatenate([v0[...], v1[...], v2r[...], vc_ref[...]], axis=0)
    i = lax.broadcasted_iota(jnp.int32, (w, 3 * w + lc), 0)
    j = lax.broadcasted_iota(jnp.int32, (w, 3 * w + lc), 1)
    valid = (jnp.abs(i + w - j) <= A_WINDOW) & ((j >= w) | (n > 0)) & ((j < 2 * w) | (n < last))
    bias = jnp.where(valid | (j >= 3 * w), 0.0, NEG_BIG)
    _gqa_heads(sink_ref, q_ref, _dup_kv(k_all), _dup_kv(v_all), bias, o_ref)


def _attn_a_lat(lay, qk, p, sink, cache_k, cache_v, layer):
    w = ATTN_Q_BLOCK
    nb = lay.tl // w
    base = lay.nctx // w
    lc = cache_k.shape[2]

    def row(off):
        return lambda b, n: base + b * nb + jnp.clip(n + off, 0, nb - 1)

    def kspec(off):
        r = row(off)
        return pl.BlockSpec((w, LANE), lambda b, n: (r(b, n), E_AK // LANE))

    def vspec(off):
        r = row(off)
        return pl.BlockSpec((w, LANE), lambda b, n: (r(b, n), E_AV // LANE))

    cspec = pl.BlockSpec((None, None, lc, LANE), lambda b, n: (b, layer, 0, 0))
    r0 = row(0)
    return pl.pallas_call(
        _attn_a_lat_body,
        grid=(lay.bl, nb),
        in_specs=[pl.BlockSpec(memory_space=pltpu.SMEM),
                  pl.BlockSpec((w, 512), lambda b, n: (r0(b, n), 0)),
                  kspec(-1), kspec(0), kspec(1), vspec(-1), vspec(0), vspec(1), cspec, cspec],
        out_specs=pl.BlockSpec((w, 512), lambda b, n: (b * nb + n, 0)),
        out_shape=jax.ShapeDtypeStruct((lay.bl * lay.tl, 512), F32),
        compiler_params=_params(2),
        name="attn_a_lat",
    )(sink, qk, qk, qk, qk, p, p, p,
      cache_k.reshape(cache_k.shape[:3] + (LANE,)), cache_v.reshape(cache_v.shape[:3] + (LANE,)))


def _gla_body(*refs, reverse, n_ctx_blocks, per_seq):
    if reverse:
        (bq_ref, bk_ref, bv_ref, bg_ref, gw_ref, gb_ref, s0_ref, tri_ref, of_ref, br_ref, bon_ref,
         o_ref, sfin_ref, st) = refs
    else:
        bq_ref, bk_ref, bv_ref, bg_ref, gw_ref, gb_ref, s0_ref, tri_ref, o_ref, sfin_ref, st = refs
    i = pl.program_id(0)
    is_start = (i < n_ctx_blocks) | (((i - n_ctx_blocks) % per_seq) == 0)

    @pl.when(is_start)
    def _():
        st[...] = s0_ref[...]

    z = _dot_hi(bg_ref[...], gw_ref[...]) + gb_ref[...]
    lg = (jnp.minimum(z, 0.0) - jnp.log1p(jnp.exp(-jnp.abs(z)))) * (1.0 / B_GATE_TAU)
    q = bq_ref[...] * (B_DK ** -0.5)
    k = bk_ref[...]
    tri = tri_ref[...]
    L = B_CHUNK
    n_chunks = ROW_BLOCK // L
    lo = _lo_lanes((L, LANE))
    ri = lax.broadcasted_iota(jnp.int32, (L, L), 0)
    ci = lax.broadcasted_iota(jnp.int32, (L, L), 1)
    causal = (ri <= ci) if reverse else (ri >= ci)
    order = range(n_chunks - 1, -1, -1) if reverse else range(n_chunks)
    for c in order:
        rows = slice(c * L, (c + 1) * L)
        b = _dot_hi(tri, lg[rows])
        b_last = b[0:1] if reverse else b[L - 1:L]
        qd = q[rows] * jnp.exp(b)
        kd = k[rows] * jnp.exp(-b)
        kt = k[rows] * jnp.exp(b_last - b)
        dec = jnp.exp(b_last)
        for h in range(B_HEADS):
            cols = slice((h // 2) * LANE, (h // 2 + 1) * LANE)
            mine = lo if h % 2 == 0 else jnp.logical_not(lo)
            qm = jnp.where(mine, qd[:, cols], 0.0)
            a = jnp.where(causal, _dot_nt(qm, kd[:, cols]), 0.0)
            vh = bv_ref[rows, h * B_DV:(h + 1) * B_DV]
            s_t = st[h]
            o = _dot(a, vh) + _dot_nt(qm, s_t)
            kv_t = _dot(vh.T, jnp.where(mine, kt[:, cols], 0.0))
            st[h] = s_t * dec[:, cols] + kv_t
            if reverse:
                tot = of_ref[rows, h * B_DV:(h + 1) * B_DV] + o
                ms = jnp.mean(tot * tot, axis=-1, keepdims=True)
                brh = br_ref[rows, h * B_DV:(h + 1) * B_DV]
                o = tot * lax.rsqrt(ms + RMS_EPS) * bon_ref[...] * (brh * jax.nn.sigmoid(brh))
            o_ref[rows, h * B_DV:(h + 1) * B_DV] = o
    sfin_ref[...] = st[...]


def _gla(lay, p, gw_pad, gb, s0, reverse, o_fwd=None, b_on=None):
    n_ctx_blocks = lay.nctx // ROW_BLOCK
    per_seq = lay.tl // ROW_BLOCK
    n_blocks = lay.nt // ROW_BLOCK

    def blk(i):
        j = i - n_ctx_blocks
        within = j % per_seq
        lat = n_ctx_blocks + (j - within) + (per_seq - 1 - within if reverse else within)
        return jnp.where(i < n_ctx_blocks, i, lat)

    def seq(i):
        return jnp.where(i < n_ctx_blocks, i, n_ctx_blocks + (i - n_ctx_blocks) // per_seq)

    def s0_idx(i):
        return jnp.where(i < n_ctx_blocks, lay.bl, (i - n_ctx_blocks) // per_seq)

    ri = jnp.arange(B_CHUNK)
    tri = ((ri[:, None] <= ri[None, :]) if reverse else (ri[:, None] >= ri[None, :])).astype(F32)
    in_specs = [pl.BlockSpec((ROW_BLOCK, 256), lambda i: (blk(i), E_BQ // 256)),
                pl.BlockSpec((ROW_BLOCK, 256), lambda i: (blk(i), E_BK // 256)),
                pl.BlockSpec((ROW_BLOCK, 512), lambda i: (blk(i), E_BV // 512)),
                pl.BlockSpec((ROW_BLOCK, LANE), lambda i: (blk(i), E_BG // LANE)),
                pl.BlockSpec((LANE, 256), lambda i: (0, 0)),
                pl.BlockSpec((1, 256), lambda i: (0, 0)),
                pl.BlockSpec((None, B_HEADS, LANE, LANE), lambda i: (s0_idx(i), 0, 0, 0)),
                pl.BlockSpec((B_CHUNK, B_CHUNK), lambda i: (0, 0))]
    args = [p, p, p, p, gw_pad, gb.reshape(1, 256), s0, tri]
    if reverse:
        in_specs += [pl.BlockSpec((ROW_BLOCK, 512), lambda i: (blk(i), 0)),
                     pl.BlockSpec((ROW_BLOCK, 512), lambda i: (blk(i), E_BR // 512)),
                     pl.BlockSpec((1, B_DV), lambda i: (0, 0))]
        args += [o_fwd, p, b_on.reshape(1, B_DV)]
    n_seq = n_ctx_blocks + lay.bl
    return pl.pallas_call(
        functools.partial(_gla_body, reverse=reverse, n_ctx_blocks=n_ctx_blocks, per_seq=per_seq),
        grid=(n_blocks,),
        in_specs=in_specs,
        out_specs=[pl.BlockSpec((ROW_BLOCK, 512), lambda i: (blk(i), 0)),
                   pl.BlockSpec((None, B_HEADS, LANE, LANE), lambda i: (seq(i), 0, 0, 0))],
        out_shape=[jax.ShapeDtypeStruct((lay.nt, 512), F32),
                   jax.ShapeDtypeStruct((n_seq, B_HEADS, LANE, LANE), F32)],
        scratch_shapes=[pltpu.VMEM((B_HEADS, LANE, LANE), F32)],
        compiler_params=_params(1),
        name="gla_bwd" if reverse else "gla_fwd",
    )(*args)


def _state_in(s):
    st = jnp.swapaxes(s.astype(F32), 2, 3)
    z = jnp.zeros_like(st)
    even = jnp.concatenate([st, z], axis=-1)
    odd = jnp.concatenate([z, st], axis=-1)
    pick = (jnp.arange(B_HEADS) % 2 == 0)[None, :, None, None]
    full = jnp.where(pick, even, odd)
    return jnp.concatenate([full, jnp.zeros_like(full[:1])], axis=0)


def _state_out(s_t):
    even = s_t[..., :B_DK]
    odd = s_t[..., B_DK:]
    pick = (jnp.arange(B_HEADS) % 2 == 0)[None, :, None, None]
    return jnp.swapaxes(jnp.where(pick, even, odd), 2, 3)


def _linear_res_body(*refs, n_in):
    a_refs, w_refs = refs[:n_in], refs[n_in:2 * n_in]
    gate_ref, res_ref, o_ref = refs[2 * n_in:]
    acc = _dot(a_refs[0][...], w_refs[0][...])
    for a_ref, w_ref in zip(a_refs[1:], w_refs[1:]):
        acc = acc + _dot(a_ref[...], w_ref[...])
    o_ref[...] = res_ref[...] + gate_ref[0] * acc


def _linear_res(lay, xs, ws, mod, chunk, res):
    n_in = len(xs)
    in_specs = [pl.BlockSpec((LIN_BLOCK, a.shape[1]), lambda i: (i, 0)) for a in xs]
    in_specs += [pl.BlockSpec(w.shape, lambda i: (0, 0)) for w in ws]
    in_specs += [_mod_spec(lay, LIN_BLOCK, chunk),
                 pl.BlockSpec((LIN_BLOCK, D_MODEL), lambda i: (i, 0))]
    return pl.pallas_call(
        functools.partial(_linear_res_body, n_in=n_in),
        grid=(lay.nt // LIN_BLOCK,),
        in_specs=in_specs,
        out_specs=pl.BlockSpec((LIN_BLOCK, D_MODEL), lambda i: (i, 0)),
        out_shape=jax.ShapeDtypeStruct((lay.nt, D_MODEL), F32),
        compiler_params=_params(1),
        name="linear_res",
    )(*xs, *ws, mod, res)


def _diff_attn_body(*refs, lam_init, has_cache):
    if has_cache:
        lamv_ref, con_ref, q_ref, k_ref, v_ref, kc_ref, vc_ref, o_ref = refs
    else:
        lamv_ref, con_ref, q_ref, k_ref, v_ref, o_ref = refs
    lv = lamv_ref[...]
    lam = (jnp.exp(jnp.sum(lv[0:1] * lv[1:2], axis=-1, keepdims=True))
           - jnp.exp(jnp.sum(lv[2:3] * lv[3:4], axis=-1, keepdims=True)) + lam_init)
    q = q_ref[...] * ATTN_SCALE
    tq = q.shape[0]
    lo = _lo_lanes(q.shape)
    qs = jnp.concatenate([jnp.where(lo, q, 0.0), jnp.where(lo, 0.0, q)], axis=0)
    s = _dot_nt(qs, k_ref[...])
    mx = jnp.max(s, axis=-1, keepdims=True)
    if has_cache:
        sc = _dot_nt(qs, kc_ref[...])
        mx = jnp.maximum(mx, jnp.max(sc, axis=-1, keepdims=True))
    p = jnp.exp(s - mx)
    den = jnp.sum(p, axis=-1, keepdims=True)
    if has_cache:
        pc = jnp.exp(sc - mx)
        den = den + jnp.sum(pc, axis=-1, keepdims=True)
    inv = 1.0 / den
    w0 = inv[:tq]
    w1 = lam * inv[tq:]
    o = _dot(p[:tq] * w0 - p[tq:] * w1, v_ref[...])
    if has_cache:
        o = o + _dot(pc[:tq] * w0 - pc[tq:] * w1, vc_ref[...])
    ms = jnp.mean(o * o, axis=-1, keepdims=True)
    o_ref[...] = o * lax.rsqrt(ms + RMS_EPS) * con_ref[...] * (1.0 - lam_init)


def _diff_attn(qk, p, lamv, c_on, lam_init, n_seq, t_seq, row_base, tq, cache=None):
    nq = t_seq // tq
    qbase = row_base // tq
    kbase = row_base // t_seq
    in_specs = [pl.BlockSpec((8, LANE), lambda b, h, i: (0, 0)),
                pl.BlockSpec((1, LANE), lambda b, h, i: (0, 0)),
                pl.BlockSpec((tq, LANE), lambda b, h, i: (qbase + b * nq + i, h)),
                pl.BlockSpec((t_seq, LANE), lambda b, h, i: (kbase + b, C_HEADS + h)),
                pl.BlockSpec((t_seq, LANE), lambda b, h, i: (kbase + b, 2 * C_HEADS + h))]
    args = [lamv, c_on.reshape(1, C_DV), qk, qk, p]
    if cache is not None:
        ck, cv, layer = cache
        lc = ck.shape[2]
        in_specs += [pl.BlockSpec((None, None, lc, LANE), lambda b, h, i: (b, layer, 0, h)),
                     pl.BlockSpec((None, None, lc, LANE), lambda b, h, i: (b, layer, 0, h))]
        args += [ck.reshape(ck.shape[:3] + (C_HEADS * LANE,)), cv.reshape(cv.shape[:3] + (C_HEADS * LANE,))]
    return pl.pallas_call(
        functools.partial(_diff_attn_body, lam_init=lam_init, has_cache=cache is not None),
        grid=(n_seq, C_HEADS, nq),
        in_specs=in_specs,
        out_specs=pl.BlockSpec((tq, LANE), lambda b, h, i: (b * nq + i, h)),
        out_shape=jax.ShapeDtypeStruct((n_seq * t_seq, C_HEADS * C_DV), F32),
        compiler_params=_params(3),
        name="diff_attn",
    )(*args)


def _top_rows(s, k, out_ref, want_rank=False):
    cur = s
    first = None
    rank = jnp.full(s.shape, float(k), F32) if want_rank else None
    for r in range(k):
        m = jnp.max(cur, axis=0, keepdims=True)
        out_ref[r:r + 1, :] = m
        first = m if first is None else first
        hit = cur == m
        if want_rank:
            rank = jnp.where(hit, float(r), rank)
        cur = jnp.where(hit, NEG_BIG, cur)
    return first, m, rank


_CAND_COUNTS = tuple(P_TOPK // (a + 1) for a in range(P_TOPK))
_CAND_ROWS = SUBLANES * ((sum(_CAND_COUNTS) + SUBLANES - 1) // SUBLANES)


def _peer_select(x_ref, g_ref, sc_ref, sh_ref, wq_ref, sk_ref,
                 h_s, q_s, r0_s, e0_s, n1_s, e1_s, top0_s, top1_s, cand_s):
    h = _rms_mod(x_ref[...], g_ref[...], sc_ref[0], sh_ref[0])
    h_s[...] = h.T.astype(BF16)
    q_s[...] = _dot(h, wq_ref[...])

    def head(hd, carry):
        qp = q_s[:, pl.ds(pl.multiple_of(hd * LANE, LANE), LANE)]
        s_t = lax.dot_general(sk_ref[hd], qp, (((1,), (1,)), ((), ())), precision=HI,
                              preferred_element_type=F32)
        for t in range(s_t.shape[1] // LANE):
            cols = slice(t * LANE, (t + 1) * LANE)
            s0 = s_t[:P_NKEYS, cols]
            s1 = s_t[P_NKEYS:, cols]
            max0, _, r0 = _top_rows(s0, P_TOPK, top0_s, want_rank=True)
            max1, _, _ = _top_rows(s1, P_TOPK, top1_s)
            row = 0
            for a, cnt in enumerate(_CAND_COUNTS):
                cand_s[row:row + cnt, :] = top0_s[a:a + 1, :] + top1_s[0:cnt, :]
                row += cnt
            cand_s[row:, :] = jnp.full((_CAND_ROWS - row, LANE), NEG_BIG, F32)
            cand = cand_s[...]
            _, thr, _ = _top_rows(cand, P_TOPK, top1_s)
            mx = max0 + max1
            z = jnp.sum(jnp.where(cand >= thr, jnp.exp(cand - mx), 0.0), axis=0, keepdims=True)
            n1 = jnp.zeros(s1.shape, F32)
            for a in range(P_TOPK // 2):
                n1 = n1 + jnp.where((top0_s[a:a + 1, :] + s1) >= thr, 1.0, 0.0)
            n_hi = jnp.zeros(max1.shape, F32)
            for a in range(P_TOPK // 2, P_TOPK):
                n_hi = n_hi + jnp.where((top0_s[a:a + 1, :] + max1) >= thr, 1.0, 0.0)
            n1 = n1 + jnp.where(s1 == max1, n_hi, 0.0)
            r0_s[hd, :, cols] = r0
            e0_s[hd, :, cols] = jnp.exp(s0 - max0)
            n1_s[hd, t] = n1.astype(BF16)
            e1_s[hd, t] = (jnp.exp(s1 - max1) / z).astype(BF16)
        return carry

    lax.fori_loop(0, P_HEADS, head, 0)


def _bf16_pair_words(x):
    bits = lax.bitcast_convert_type(x.astype(BF16).astype(F32), jnp.uint32)
    return bits | (bits >> 16)


def _packed_row(w):
    return pltpu.bitcast(jnp.broadcast_to(w, (GATE_ROWS // 2, LANE)), BF16)


def _peer_step(group, u_ref, vt_ref, h_s, r0_s, e0_s, r0c_s, e0c_s, n1_s, e1_s,
               st_score, st_gate, wt_gate, wt_acc, acc_s, *, do_acc, do_gate, do_score):
    tn = h_s.shape[1]
    n_t = tn // LANE
    kc = u_ref.shape[0] // n_t
    if do_gate:
        r0c_s[...] = _bf16_pair_words(r0_s[:, group, :])
        e0c_s[...] = _bf16_pair_words(e0_s[:, group, :])
    zero = jnp.zeros((GATE_ROWS, LANE), BF16)
    for t in range(n_t):
        cols = slice(t * LANE, (t + 1) * LANE)
        ks = slice(t * kc, (t + 1) * kc)
        for jq in range(P_NKEYS // GATE_ROWS):
            jr = slice(jq * GATE_ROWS, (jq + 1) * GATE_ROWS)
            g = [zero for _ in range(SUBLANES)]
            for hd in range(P_HEADS if do_gate else 0):
                n1q = n1_s[hd, t, jr, :]
                e1q = e1_s[hd, t, jr, :]
                for il in range(SUBLANES):
                    r0r = _packed_row(r0c_s[hd, il:il + 1, cols])
                    e0r = _packed_row(e0c_s[hd, il:il + 1, cols])
                    g[il] = g[il] + e0r * jnp.where(r0r < n1q, e1q, zero)
            for il in range(SUBLANES if do_gate else 0):
                rows = slice(il * P_NKEYS + jq * GATE_ROWS, il * P_NKEYS + (jq + 1) * GATE_ROWS)
                sc = st_gate[t, rows, :].astype(BF16)
                act = (0.5 * sc) * (1.0 + lax.erf(sc * (2.0 ** -0.5)))
                wt_gate[rows, cols] = g[il] * act
            if jq == 1 and do_acc:
                acc_s[...] += _dot(vt_ref[:, ks], wt_acc[ks, :])
            if jq == 3 and do_score:
                s_new = _dot(u_ref[ks, :], h_s[...])
                for tt in range(n_t):
                    st_score[tt, ks, :] = s_new[:, tt * LANE:(tt + 1) * LANE]


def _peer_body(x_ref, g_ref, sc_ref, sh_ref, gate_ref, wq_ref, sk_ref, u_ref, vt_ref, o_ref,
               h_s, q_s, r0_s, e0_s, r0c_s, e0c_s, n1_s, e1_s, top0_s, top1_s, cand_s,
               st0_s, st1_s, wt0_s, wt1_s, acc_s, *, n_eb):
    e = pl.program_id(1)
    assert u_ref.shape[0] == SUBLANES * P_NKEYS
    assert n_eb >= 2

    @pl.when(e == 0)
    def _():
        _peer_select(x_ref, g_ref, sc_ref, sh_ref, wq_ref, sk_ref,
                     h_s, q_s, r0_s, e0_s, n1_s, e1_s, top0_s, top1_s, cand_s)
        acc_s[...] = jnp.zeros_like(acc_s)

    gate_block = jnp.clip(e - 1, 0, n_eb - 1)
    group = pl.ds(pl.multiple_of(gate_block * SUBLANES, SUBLANES), SUBLANES)
    st = (st0_s, st1_s)
    wt = (wt0_s, wt1_s)

    def step(cond, slot, **parts):
        @pl.when(cond)
        def _():
            _peer_step(group, u_ref, vt_ref, h_s, r0_s, e0_s, r0c_s, e0c_s, n1_s, e1_s,
                       st[slot], st[1 - slot], wt[1 - slot], wt[slot], acc_s, **parts)

    step(e == 0, 0, do_acc=False, do_gate=False, do_score=True)
    step(e == 1, 1, do_acc=False, do_gate=True, do_score=True)
    for slot in range(2):
        step((e >= 2) & (e < n_eb) & (e % 2 == slot), slot, do_acc=True, do_gate=True, do_score=True)
    step(e == n_eb, n_eb % 2, do_acc=True, do_gate=True, do_score=False)
    step(e == n_eb + 1, (n_eb + 1) % 2, do_acc=True, do_gate=False, do_score=False)

    @pl.when(e == n_eb + 1)
    def _():
        o_ref[...] = x_ref[...] + gate_ref[0] * acc_s[...].T


def _peer(lay, x, g, mod, wq, sk2, u, vt):
    tn, en = PEER_TOKENS, PEER_EXPERTS
    n_eb = u.shape[0] // en
    grid = (lay.nt // tn, n_eb + 2)
    head_scr = pltpu.VMEM((P_HEADS, P_NKEYS, tn), F32)
    rows_scr = pltpu.VMEM((P_HEADS, SUBLANES, tn), jnp.uint32)
    tile_scr = pltpu.VMEM((P_HEADS, tn // LANE, P_NKEYS, LANE), BF16)
    score_scr = pltpu.VMEM((tn // LANE, en, LANE), F32)
    gated_scr = pltpu.VMEM((en, tn), BF16)
    return pl.pallas_call(
        functools.partial(_peer_body, n_eb=n_eb),
        grid=grid,
        in_specs=[pl.BlockSpec((tn, D_MODEL), lambda i, e: (i, 0)),
                  pl.BlockSpec((1, D_MODEL), lambda i, e: (0, 0)),
                  _mod_spec(lay, tn, 4), _mod_spec(lay, tn, 3), _mod_spec(lay, tn, 5),
                  pl.BlockSpec((D_MODEL, P_HEADS * LANE), lambda i, e: (0, 0)),
                  pl.BlockSpec((P_HEADS, 2 * P_NKEYS, LANE), lambda i, e: (0, 0, 0)),
                  pl.BlockSpec((en, D_MODEL), lambda i, e: (jnp.minimum(e, n_eb - 1), 0)),
                  pl.BlockSpec((D_MODEL, en), lambda i, e: (0, jnp.clip(e - 2, 0, n_eb - 1)))],
        out_specs=pl.BlockSpec((tn, D_MODEL), lambda i, e: (i, 0)),
        out_shape=jax.ShapeDtypeStruct((lay.nt, D_MODEL), F32),
        scratch_shapes=[pltpu.VMEM((D_MODEL, tn), BF16),
                        pltpu.VMEM((tn, P_HEADS * LANE), F32),
                        head_scr, head_scr, rows_scr, rows_scr, tile_scr, tile_scr,
                        pltpu.VMEM((P_TOPK, LANE), F32),
                        pltpu.VMEM((P_TOPK, LANE), F32),
                        pltpu.VMEM((_CAND_ROWS, LANE), F32),
                        score_scr, score_scr, gated_scr, gated_scr,
                        pltpu.VMEM((D_MODEL, tn), F32)],
        compiler_params=_params(2),
        name="peer",
    )(x, g.reshape(1, D_MODEL), mod, mod, mod, wq, sk2, u, vt)


def _sub_key_blocks(sub_keys):
    z = jnp.zeros_like(sub_keys[:, 0])
    top = jnp.concatenate([sub_keys[:, 0], z], axis=-1)
    bot = jnp.concatenate([z, sub_keys[:, 1]], axis=-1)
    return jnp.concatenate([top, bot], axis=1)


def _lambda_init(layer):
    return 0.8 - 0.6 * math.exp(-0.3 * layer)


def _even_w_in(w):
    aq, ak, av, bq, bk, bv, br, bg = jnp.split(w, [512, 640, 768, 1024, 1280, 1792, 2304], axis=1)
    pad = jnp.zeros((w.shape[0], E_WIDTH - E_BG - bg.shape[1]), w.dtype)
    return jnp.concatenate([aq, ak, av, bq, bv, br, bk, bg, pad], axis=1).astype(BF16)


def kernel(x_prompt, x_sample, cache_a_k, cache_a_v, state_b_fwd, state_b_bwd, cache_c_k, cache_c_v, c, c_ctx, ada_w, ada_b, norm_mix_g, norm_ffn_g, e_w_in, e_w_out, a_q_norm, a_k_norm, a_sink, b_gate_w_f, b_gate_b_f, b_gate_w_b, b_gate_b_b, b_out_norm, o_w_in, o_w_out, c_q_norm, c_k_norm, c_lambda_q1, c_lambda_k1, c_lambda_q2, c_lambda_k2, c_out_norm, p_w_q, p_sub_keys, p_u, p_v):
    bc, tc, d = x_prompt.shape
    bl, tl, _ = x_sample.shape
    depth = ada_w.shape[0]
    lay = _Layout(bc, tc, bl, tl)
    n_mod_rows = 8 * ((bl + 1 + 7) // 8)
    cvecs = jnp.concatenate([c, c_ctx[None, :], jnp.zeros((n_mod_rows - bl - 1, d), F32)], axis=0)
    mods = _modulation(cvecs, ada_w, ada_b)
    cos, sin = _rope_tables(tl)
    x = jnp.concatenate([x_prompt.reshape(lay.nctx, d), x_sample.reshape(bl * tl, d)], axis=0)

    ak_l, av_l, sf_l, sb_l, ck_l, cv_l = [], [], [], [], [], []
    for l in range(depth):
        i = l // 2
        mod = mods[l].reshape(n_mod_rows, 1, N_MOD * d)
        if l % 2 == 0:
            p = _ln_proj(lay, x, norm_mix_g[l], mod, 0, _even_w_in(e_w_in[i]))
            gains = jnp.concatenate([jnp.tile(a_q_norm[i], A_HEADS), jnp.tile(a_k_norm[i], A_KV_HEADS)])
            qk = _qk_prep(lay, p, E_QK_WIDTH, gains.reshape(1, E_QK_WIDTH), cos, sin)
            sink = a_sink[i].reshape(A_HEADS)
            oa = jnp.concatenate([_attn_a_ctx(lay, qk, p, sink),
                                  _attn_a_lat(lay, qk, p, sink, cache_a_k, cache_a_v, i)], axis=0)
            zrow = jnp.zeros((LANE - 2 * B_GATE_RANK, 256), F32)
            zgate = jnp.zeros((B_GATE_RANK, 256), F32)
            gw_f = jnp.concatenate([b_gate_w_f[i], zgate, zrow], axis=0)
            gw_b = jnp.concatenate([zgate, b_gate_w_b[i], zrow], axis=0)
            o_f, s_f = _gla(lay, p, gw_f, b_gate_b_f[i], _state_in(state_b_fwd[:, i]), False)
            ob, s_b = _gla(lay, p, gw_b, b_gate_b_b[i], _state_in(state_b_bwd[:, i]), True,
                           o_fwd=o_f, b_on=b_out_norm[i])
            w_out = e_w_out[i].astype(BF16)
            x = _linear_res(lay, [oa, ob], [w_out[:512], w_out[512:]], mod, 2, x)
            ak_l.append(qk[:lay.nctx, E_AK:E_AK + LANE].reshape(bc, tc, A_KV_HEADS, HEAD_DIM))
            av_l.append(p[:lay.nctx, E_AV:E_AV + LANE].reshape(bc, tc, A_KV_HEADS, HEAD_DIM))
            sf_l.append(_state_out(s_f[:bc]))
            sb_l.append(_state_out(s_b[:bc]))
        else:
            lam_init = _lambda_init(l)
            p = _ln_proj(lay, x, norm_mix_g[l], mod, 0, o_w_in[i].astype(BF16))
            gains = jnp.concatenate([jnp.tile(c_q_norm[i], 2 * C_HEADS), jnp.tile(c_k_norm[i], 2 * C_HEADS)])
            qk = _qk_prep(lay, p, O_QK_WIDTH, gains.reshape(1, O_QK_WIDTH), cos, sin)
            lamv = jnp.stack([c_lambda_q1[i], c_lambda_k1[i], c_lambda_q2[i], c_lambda_k2[i]])
            lamv = jnp.pad(lamv, ((0, 4), (0, LANE - HEAD_DIM)))
            o_ctx = _diff_attn(qk, p, lamv, c_out_norm[i], lam_init, bc, tc, 0, tc)
            o_lat = _diff_attn(qk, p, lamv, c_out_norm[i], lam_init, bl, tl, lay.nctx, DIFF_Q_BLOCK,
                               cache=(cache_c_k, cache_c_v, i))
            o = jnp.concatenate([o_ctx, o_lat], axis=0)
            x = _linear_res(lay, [o], [o_w_out[i].astype(BF16)], mod, 2, x)
            ck_l.append(qk[:lay.nctx, 1024:2048].reshape(bc, tc, C_HEADS, 2, HEAD_DIM))
            cv_l.append(p[:lay.nctx, 2048:3072].reshape(bc, tc, C_HEADS, C_DV))
        x = _peer(lay, x, norm_ffn_g[l], mod, p_w_q[l].astype(BF16), _sub_key_blocks(p_sub_keys[l]),
                  p_u[l].astype(BF16), p_v[l].T.astype(BF16))

    return (x[:lay.nctx].reshape(bc, tc, d), x[lay.nctx:].reshape(bl, tl, d),
            jnp.stack(ak_l, axis=1), jnp.stack(av_l, axis=1), jnp.stack(sf_l, axis=1), jnp.stack(sb_l, axis=1),
            jnp.stack(ck_l, axis=1), jnp.stack(cv_l, axis=1))
```

```python
import functools
import math

import jax
import jax.numpy as jnp
from jax import lax
from jax.experimental import pallas as pl
from jax.experimental.pallas import tpu as pltpu

F32 = jnp.float32
BF16 = jnp.bfloat16
HI = lax.Precision.HIGHEST

D_MODEL = 1024
GRID_W = 64
HEAD_DIM = 64
ATTN_SCALE = HEAD_DIM ** -0.5
ROPE_BASE = 10000.0
RMS_EPS = 1e-6
N_MOD = 6
A_HEADS = 8
A_KV_HEADS = 2
A_WINDOW = 128
B_HEADS = 4
B_DK = 64
B_DV = 128
B_GATE_RANK = 16
B_GATE_TAU = 16.0
B_CHUNK = 64
C_HEADS = 8
C_DV = 128
P_HEADS = 8
P_NKEYS = 128
P_TOPK = 16

LANE = 128
SUBLANES = 8
ROW_BLOCK = 256
LIN_BLOCK = 512
PEER_TOKENS = 512
PEER_EXPERTS = SUBLANES * P_NKEYS
MXU_PIECE_ROWS = 128
GATE_ROWS = 32
ATTN_Q_BLOCK = 128
DIFF_Q_BLOCK = 256
NEG_BIG = -1e30
VMEM_LIMIT = 56 * 1024 * 1024

E_AQ, E_AK, E_AV, E_BQ, E_BV, E_BR, E_BK, E_BG, E_WIDTH = 0, 512, 640, 768, 1024, 1536, 2048, 2304, 2432
E_QK_WIDTH = 640
O_QK_WIDTH = 2048


def _dot(a, b):
    return lax.dot_general(a.astype(BF16), b.astype(BF16), (((1,), (0,)), ((), ())),
                           preferred_element_type=F32)


def _dot_nt(a, b):
    return lax.dot_general(a.astype(BF16), b.astype(BF16), (((1,), (1,)), ((), ())),
                           preferred_element_type=F32)


def _dot_hi(a, b):
    return lax.dot_general(a, b, (((1,), (0,)), ((), ())), precision=HI, preferred_element_type=F32)


def _params(n_axes, flags=None):
    return pltpu.CompilerParams(dimension_semantics=("arbitrary",) * n_axes,
                                vmem_limit_bytes=VMEM_LIMIT, flags=flags)


def _lo_lanes(shape):
    return (lax.broadcasted_iota(jnp.int32, shape, len(shape) - 1) & 64) == 0


def _mod_body(c_ref, w_ref, b_ref, o_ref):
    c = c_ref[...]
    s = c * jax.nn.sigmoid(c)
    o_ref[0] = _dot_hi(s, w_ref[0]) + b_ref[0]


def _modulation(cvecs, ada_w, ada_b):
    depth, d, n = ada_w.shape
    r = cvecs.shape[0]
    tn = 1536
    return pl.pallas_call(
        _mod_body,
        grid=(depth, n // tn),
        in_specs=[pl.BlockSpec((r, d), lambda l, j: (0, 0)),
                  pl.BlockSpec((1, d, tn), lambda l, j: (l, 0, j)),
                  pl.BlockSpec((1, 1, tn), lambda l, j: (l, 0, j))],
        out_specs=pl.BlockSpec((1, r, tn), lambda l, j: (l, 0, j)),
        out_shape=jax.ShapeDtypeStruct((depth, r, n), F32),
        compiler_params=_params(2),
        name="modulation",
    )(cvecs, ada_w, ada_b.reshape(depth, 1, n))


class _Layout:
    def __init__(self, bc, tc, bl, tl):
        self.bc, self.tc, self.bl, self.tl = bc, tc, bl, tl
        self.nctx = bc * tc
        self.nt = self.nctx + bl * tl
        assert tc == ROW_BLOCK and tl % LIN_BLOCK == 0 and self.nctx % LIN_BLOCK == 0
        assert self.nctx % tl == 0 and tl % GRID_W == 0 and self.nt % PEER_TOKENS == 0
        assert self.nctx % PEER_TOKENS == 0 and tl % PEER_TOKENS == 0

    def mod_row(self, i, block):
        nctx_blocks = self.nctx // block
        return jnp.where(i < nctx_blocks, self.bl, (i - nctx_blocks) // (self.tl // block))


def _mod_spec(lay, block, chunk):
    return pl.BlockSpec((1, 1, D_MODEL), lambda i, *_: (lay.mod_row(i, block), 0, chunk))


def _rms_mod(x, g, sc, sh):
    ms = jnp.mean(x * x, axis=-1, keepdims=True)
    return (x * lax.rsqrt(ms + RMS_EPS) * g) * (1.0 + sc) + sh


def _ln_proj_body(x_ref, g_ref, sc_ref, sh_ref, w_ref, o_ref):
    h = _rms_mod(x_ref[...], g_ref[...], sc_ref[0], sh_ref[0])
    o_ref[...] = _dot(h, w_ref[...])


def _ln_proj(lay, x, g, mod, chunk, w):
    n_out = w.shape[1]
    return pl.pallas_call(
        _ln_proj_body,
        grid=(lay.nt // LIN_BLOCK,),
        in_specs=[pl.BlockSpec((LIN_BLOCK, D_MODEL), lambda i: (i, 0)),
                  pl.BlockSpec((1, D_MODEL), lambda i: (0, 0)),
                  _mod_spec(lay, LIN_BLOCK, chunk + 1),
                  _mod_spec(lay, LIN_BLOCK, chunk),
                  pl.BlockSpec((D_MODEL, n_out), lambda i: (0, 0))],
        out_specs=pl.BlockSpec((LIN_BLOCK, n_out), lambda i: (i, 0)),
        out_shape=jax.ShapeDtypeStruct((lay.nt, n_out), F32),
        compiler_params=_params(1),
        name="ln_proj",
    )(x, g.reshape(1, D_MODEL), mod, mod, w)


def _prep_body(x_ref, g_ref, cos_ref, sin_ref, bd_ref, o_ref, *, n_chunks):
    c = cos_ref[...]
    s = sin_ref[...]
    first = (lax.broadcasted_iota(jnp.int32, c.shape, 1) & 16) == 0
    for j in range(n_chunks):
        cols = slice(j * LANE, (j + 1) * LANE)
        x = x_ref[:, cols]
        ms = _dot_hi(x * x, bd_ref[...])
        y = x * lax.rsqrt(ms + RMS_EPS) * g_ref[:, cols]
        partner = jnp.where(first, pltpu.roll(y, LANE - 16, 1), pltpu.roll(y, 16, 1))
        o_ref[:, cols] = y * c + partner * s


def _rope_tables(tl):
    t = jnp.arange(tl)
    row = (t // GRID_W).astype(F32)
    col = (t % GRID_W).astype(F32)
    nf = HEAD_DIM // 4
    freqs = ROPE_BASE ** (-jnp.arange(nf, dtype=F32) / nf)
    ang_r = row[:, None] * freqs
    ang_c = col[:, None] * freqs
    cos64 = jnp.concatenate([jnp.cos(ang_r), jnp.cos(ang_r), jnp.cos(ang_c), jnp.cos(ang_c)], axis=1)
    sin64 = jnp.concatenate([-jnp.sin(ang_r), jnp.sin(ang_r), -jnp.sin(ang_c), jnp.sin(ang_c)], axis=1)
    cos = jnp.concatenate([jnp.ones((ROW_BLOCK, HEAD_DIM), F32), cos64], axis=0)
    sin = jnp.concatenate([jnp.zeros((ROW_BLOCK, HEAD_DIM), F32), sin64], axis=0)
    return jnp.tile(cos, (1, 2)), jnp.tile(sin, (1, 2))


def _qk_prep(lay, p, width, gains, cos, sin):
    n_ctx_blocks = lay.nctx // ROW_BLOCK
    per_seq = lay.tl // ROW_BLOCK

    def tab(i):
        return (jnp.where(i < n_ctx_blocks, 0, 1 + (i - n_ctx_blocks) % per_seq), 0)

    seg = jnp.arange(LANE) // HEAD_DIM
    bd = (seg[:, None] == seg[None, :]).astype(F32) / HEAD_DIM
    return pl.pallas_call(
        functools.partial(_prep_body, n_chunks=width // LANE),
        grid=(lay.nt // ROW_BLOCK,),
        in_specs=[pl.BlockSpec((ROW_BLOCK, width), lambda i: (i, 0)),
                  pl.BlockSpec((1, width), lambda i: (0, 0)),
                  pl.BlockSpec((ROW_BLOCK, LANE), tab),
                  pl.BlockSpec((ROW_BLOCK, LANE), tab),
                  pl.BlockSpec((LANE, LANE), lambda i: (0, 0))],
        out_specs=pl.BlockSpec((ROW_BLOCK, width), lambda i: (i, 0)),
        out_shape=jax.ShapeDtypeStruct((lay.nt, width), F32),
        compiler_params=_params(1),
        name="qk_prep",
    )(p, gains, cos, sin, bd)


def _dup_kv(x):
    lo = _lo_lanes(x.shape)
    xr = pltpu.roll(x, HEAD_DIM, 1)
    return jnp.where(lo, x, xr), jnp.where(lo, xr, x)


def _gqa_heads(sink_ref, q_ref, k2, v2, bias, o_ref):
    rows = q_ref.shape[0]
    lo = _lo_lanes((rows, LANE))
    for pair in range(A_HEADS // 2):
        kv = pair // 2
        cols = slice(pair * LANE, (pair + 1) * LANE)
        qp = q_ref[:, cols] * ATTN_SCALE
        outs = []
        for half in range(2):
            qm = jnp.where(lo, qp, 0.0) if half == 0 else jnp.where(lo, 0.0, qp)
            s = _dot_nt(qm, k2[kv])
            if bias is not None:
                s = s + bias
            sk = sink_ref[2 * pair + half]
            mx = jnp.maximum(jnp.max(s, axis=-1, keepdims=True), sk)
            p = jnp.exp(s - mx)
            den = jnp.sum(p, axis=-1, keepdims=True) + jnp.exp(sk - mx)
            outs.append(_dot(p, v2[kv]) / den)
        o_ref[:, cols] = jnp.where(lo, outs[0], outs[1])


def _attn_a_ctx_body(sink_ref, q_ref, k_ref, v_ref, o_ref):
    _gqa_heads(sink_ref, q_ref, _dup_kv(k_ref[...]), _dup_kv(v_ref[...]), None, o_ref)


def _attn_a_ctx(lay, qk, p, sink):
    return pl.pallas_call(
        _attn_a_ctx_body,
        grid=(lay.bc,),
        in_specs=[pl.BlockSpec(memory_space=pltpu.SMEM),
                  pl.BlockSpec((lay.tc, 512), lambda b: (b, 0)),
                  pl.BlockSpec((lay.tc, LANE), lambda b: (b, E_AK // LANE)),
                  pl.BlockSpec((lay.tc, LANE), lambda b: (b, E_AV // LANE))],
        out_specs=pl.BlockSpec((lay.tc, 512), lambda b: (b, 0)),
        out_shape=jax.ShapeDtypeStruct((lay.nctx, 512), F32),
        compiler_params=_params(1),
        name="attn_a_ctx",
    )(sink, qk, qk, p)


def _attn_a_lat_body(sink_ref, q_ref, k0, k1, k2r, v0, v1, v2r, kc_ref, vc_ref, o_ref):
    n = pl.program_id(1)
    last = pl.num_programs(1) - 1
    w = ATTN_Q_BLOCK
    lc = kc_ref.shape[0]
    k_all = jnp.concatenate([k0[...], k1[...], k2r[...], kc_ref[...]], axis=0)
    v_all = jnp.concatenate([v0[...], v1[...], v2r[...], vc_ref[...]], axis=0)
    i = lax.broadcasted_iota(jnp.int32, (w, 3 * w + lc), 0)
    j = lax.broadcasted_iota(jnp.int32, (w, 3 * w + lc), 1)
    valid = (jnp.abs(i + w - j) <= A_WINDOW) & ((j >= w) | (n > 0)) & ((j < 2 * w) | (n < last))
    bias = jnp.where(valid | (j >= 3 * w), 0.0, NEG_BIG)
    _gqa_heads(sink_ref, q_ref, _dup_kv(k_all), _dup_kv(v_all), bias, o_ref)


def _attn_a_lat(lay, qk, p, sink, cache_k, cache_v, layer):
    w = ATTN_Q_BLOCK
    nb = lay.tl // w
    base = lay.nctx // w
    lc = cache_k.shape[2]

    def row(off):
        return lambda b, n: base + b * nb + jnp.clip(n + off, 0, nb - 1)

    def kspec(off):
        r = row(off)
        return pl.BlockSpec((w, LANE), lambda b, n: (r(b, n), E_AK // LANE))

    def vspec(off):
        r = row(off)
        return pl.BlockSpec((w, LANE), lambda b, n: (r(b, n), E_AV // LANE))

    cspec = pl.BlockSpec((None, None, lc, LANE), lambda b, n: (b, layer, 0, 0))
    r0 = row(0)
    return pl.pallas_call(
        _attn_a_lat_body,
        grid=(lay.bl, nb),
        in_specs=[pl.BlockSpec(memory_space=pltpu.SMEM),
                  pl.BlockSpec((w, 512), lambda b, n: (r0(b, n), 0)),
                  kspec(-1), kspec(0), kspec(1), vspec(-1), vspec(0), vspec(1), cspec, cspec],
        out_specs=pl.BlockSpec((w, 512), lambda b, n: (b * nb + n, 0)),
        out_shape=jax.ShapeDtypeStruct((lay.bl * lay.tl, 512), F32),
        compiler_params=_params(2),
        name="attn_a_lat",
    )(sink, qk, qk, qk, qk, p, p, p,
      cache_k.reshape(cache_k.shape[:3] + (LANE,)), cache_v.reshape(cache_v.shape[:3] + (LANE,)))


def _gla_body(*refs, reverse, n_ctx_blocks, per_seq):
    if reverse:
        (bq_ref, bk_ref, bv_ref, bg_ref, gw_ref, gb_ref, s0_ref, tri_ref, of_ref, br_ref, bon_ref,
         o_ref, sfin_ref, st) = refs
    else:
        bq_ref, bk_ref, bv_ref, bg_ref, gw_ref, gb_ref, s0_ref, tri_ref, o_ref, sfin_ref, st = refs
    i = pl.program_id(0)
    is_start = (i < n_ctx_blocks) | (((i - n_ctx_blocks) % per_seq) == 0)

    @pl.when(is_start)
    def _():
        st[...] = s0_ref[...]

    z = _dot_hi(bg_ref[...], gw_ref[...]) + gb_ref[...]
    lg = (jnp.minimum(z, 0.0) - jnp.log1p(jnp.exp(-jnp.abs(z)))) * (1.0 / B_GATE_TAU)
    q = bq_ref[...] * (B_DK ** -0.5)
    k = bk_ref[...]
    tri = tri_ref[...]
    L = B_CHUNK
    n_chunks = ROW_BLOCK // L
    lo = _lo_lanes((L, LANE))
    ri = lax.broadcasted_iota(jnp.int32, (L, L), 0)
    ci = lax.broadcasted_iota(jnp.int32, (L, L), 1)
    causal = (ri <= ci) if reverse else (ri >= ci)
    order = range(n_chunks - 1, -1, -1) if reverse else range(n_chunks)
    for c in order:
        rows = slice(c * L, (c + 1) * L)
        b = _dot_hi(tri, lg[rows])
        b_last = b[0:1] if reverse else b[L - 1:L]
        qd = q[rows] * jnp.exp(b)
        kd = k[rows] * jnp.exp(-b)
        kt = k[rows] * jnp.exp(b_last - b)
        dec = jnp.exp(b_last)
        for h in range(B_HEADS):
            cols = slice((h // 2) * LANE, (h // 2 + 1) * LANE)
            mine = lo if h % 2 == 0 else jnp.logical_not(lo)
            qm = jnp.where(mine, qd[:, cols], 0.0)
            a = jnp.where(causal, _dot_nt(qm, kd[:, cols]), 0.0)
            vh = bv_ref[rows, h * B_DV:(h + 1) * B_DV]
            s_t = st[h]
            o = _dot(a, vh) + _dot_nt(qm, s_t)
            kv_t = _dot(vh.T, jnp.where(mine, kt[:, cols], 0.0))
            st[h] = s_t * dec[:, cols] + kv_t
            if reverse:
                tot = of_ref[rows, h * B_DV:(h + 1) * B_DV] + o
                ms = jnp.mean(tot * tot, axis=-1, keepdims=True)
                brh = br_ref[rows, h * B_DV:(h + 1) * B_DV]
                o = tot * lax.rsqrt(ms + RMS_EPS) * bon_ref[...] * (brh * jax.nn.sigmoid(brh))
            o_ref[rows, h * B_DV:(h + 1) * B_DV] = o
    sfin_ref[...] = st[...]


def _gla(lay, p, gw_pad, gb, s0, reverse, o_fwd=None, b_on=None):
    n_ctx_blocks = lay.nctx // ROW_BLOCK
    per_seq = lay.tl // ROW_BLOCK
    n_blocks = lay.nt // ROW_BLOCK

    def blk(i):
        j = i - n_ctx_blocks
        within = j % per_seq
        lat = n_ctx_blocks + (j - within) + (per_seq - 1 - within if reverse else within)
        return jnp.where(i < n_ctx_blocks, i, lat)

    def seq(i):
        return jnp.where(i < n_ctx_blocks, i, n_ctx_blocks + (i - n_ctx_blocks) // per_seq)

    def s0_idx(i):
        return jnp.where(i < n_ctx_blocks, lay.bl, (i - n_ctx_blocks) // per_seq)

    ri = jnp.arange(B_CHUNK)
    tri = ((ri[:, None] <= ri[None, :]) if reverse else (ri[:, None] >= ri[None, :])).astype(F32)
    in_specs = [pl.BlockSpec((ROW_BLOCK, 256), lambda i: (blk(i), E_BQ // 256)),
                pl.BlockSpec((ROW_BLOCK, 256), lambda i: (blk(i), E_BK // 256)),
                pl.BlockSpec((ROW_BLOCK, 512), lambda i: (blk(i), E_BV // 512)),
                pl.BlockSpec((ROW_BLOCK, LANE), lambda i: (blk(i), E_BG // LANE)),
                pl.BlockSpec((LANE, 256), lambda i: (0, 0)),
                pl.BlockSpec((1, 256), lambda i: (0, 0)),
                pl.BlockSpec((None, B_HEADS, LANE, LANE), lambda i: (s0_idx(i), 0, 0, 0)),
                pl.BlockSpec((B_CHUNK, B_CHUNK), lambda i: (0, 0))]
    args = [p, p, p, p, gw_pad, gb.reshape(1, 256), s0, tri]
    if reverse:
        in_specs += [pl.BlockSpec((ROW_BLOCK, 512), lambda i: (blk(i), 0)),
                     pl.BlockSpec((ROW_BLOCK, 512), lambda i: (blk(i), E_BR // 512)),
                     pl.BlockSpec((1, B_DV), lambda i: (0, 0))]
        args += [o_fwd, p, b_on.reshape(1, B_DV)]
    n_seq = n_ctx_blocks + lay.bl
    return pl.pallas_call(
        functools.partial(_gla_body, reverse=reverse, n_ctx_blocks=n_ctx_blocks, per_seq=per_seq),
        grid=(n_blocks,),
        in_specs=in_specs,
        out_specs=[pl.BlockSpec((ROW_BLOCK, 512), lambda i: (blk(i), 0)),
                   pl.BlockSpec((None, B_HEADS, LANE, LANE), lambda i: (seq(i), 0, 0, 0))],
        out_shape=[jax.ShapeDtypeStruct((lay.nt, 512), F32),
                   jax.ShapeDtypeStruct((n_seq, B_HEADS, LANE, LANE), F32)],
        scratch_shapes=[pltpu.VMEM((B_HEADS, LANE, LANE), F32)],
        compiler_params=_params(1),
        name="gla_bwd" if reverse else "gla_fwd",
    )(*args)


def _state_in(s):
    st = jnp.swapaxes(s.astype(F32), 2, 3)
    z = jnp.zeros_like(st)
    even = jnp.concatenate([st, z], axis=-1)
    odd = jnp.concatenate([z, st], axis=-1)
    pick = (jnp.arange(B_HEADS) % 2 == 0)[None, :, None, None]
    full = jnp.where(pick, even, odd)
    return jnp.concatenate([full, jnp.zeros_like(full[:1])], axis=0)


def _state_out(s_t):
    even = s_t[..., :B_DK]
    odd = s_t[..., B_DK:]
    pick = (jnp.arange(B_HEADS) % 2 == 0)[None, :, None, None]
    return jnp.swapaxes(jnp.where(pick, even, odd), 2, 3)


def _linear_res_body(*refs, n_in):
    a_refs, w_refs = refs[:n_in], refs[n_in:2 * n_in]
    gate_ref, res_ref, o_ref = refs[2 * n_in:]
    acc = _dot(a_refs[0][...], w_refs[0][...])
    for a_ref, w_ref in zip(a_refs[1:], w_refs[1:]):
        acc = acc + _dot(a_ref[...], w_ref[...])
    o_ref[...] = res_ref[...] + gate_ref[0] * acc


def _linear_res(lay, xs, ws, mod, chunk, res):
    n_in = len(xs)
    in_specs = [pl.BlockSpec((LIN_BLOCK, a.shape[1]), lambda i: (i, 0)) for a in xs]
    in_specs += [pl.BlockSpec(w.shape, lambda i: (0, 0)) for w in ws]
    in_specs += [_mod_spec(lay, LIN_BLOCK, chunk),
                 pl.BlockSpec((LIN_BLOCK, D_MODEL), lambda i: (i, 0))]
    return pl.pallas_call(
        functools.partial(_linear_res_body, n_in=n_in),
        grid=(lay.nt // LIN_BLOCK,),
        in_specs=in_specs,
        out_specs=pl.BlockSpec((LIN_BLOCK, D_MODEL), lambda i: (i, 0)),
        out_shape=jax.ShapeDtypeStruct((lay.nt, D_MODEL), F32),
        compiler_params=_params(1),
        name="linear_res",
    )(*xs, *ws, mod, res)


def _diff_attn_body(*refs, lam_init, has_cache):
    if has_cache:
        lamv_ref, con_ref, q_ref, k_ref, v_ref, kc_ref, vc_ref, o_ref = refs
    else:
        lamv_ref, con_ref, q_ref, k_ref, v_ref, o_ref = refs
    lv = lamv_ref[...]
    lam = (jnp.exp(jnp.sum(lv[0:1] * lv[1:2], axis=-1, keepdims=True))
           - jnp.exp(jnp.sum(lv[2:3] * lv[3:4], axis=-1, keepdims=True)) + lam_init)
    q = q_ref[...] * ATTN_SCALE
    tq = q.shape[0]
    lo = _lo_lanes(q.shape)
    qs = jnp.concatenate([jnp.where(lo, q, 0.0), jnp.where(lo, 0.0, q)], axis=0)
    s = _dot_nt(qs, k_ref[...])
    mx = jnp.max(s, axis=-1, keepdims=True)
    if has_cache:
        sc = _dot_nt(qs, kc_ref[...])
        mx = jnp.maximum(mx, jnp.max(sc, axis=-1, keepdims=True))
    p = jnp.exp(s - mx)
    den = jnp.sum(p, axis=-1, keepdims=True)
    if has_cache:
        pc = jnp.exp(sc - mx)
        den = den + jnp.sum(pc, axis=-1, keepdims=True)
    inv = 1.0 / den
    w0 = inv[:tq]
    w1 = lam * inv[tq:]
    o = _dot(p[:tq] * w0 - p[tq:] * w1, v_ref[...])
    if has_cache:
        o = o + _dot(pc[:tq] * w0 - pc[tq:] * w1, vc_ref[...])
    ms = jnp.mean(o * o, axis=-1, keepdims=True)
    o_ref[...] = o * lax.rsqrt(ms + RMS_EPS) * con_ref[...] * (1.0 - lam_init)


def _diff_attn(qk, p, lamv, c_on, lam_init, n_seq, t_seq, row_base, tq, cache=None):
    nq = t_seq // tq
    qbase = row_base // tq
    kbase = row_base // t_seq
    in_specs = [pl.BlockSpec((8, LANE), lambda b, h, i: (0, 0)),
                pl.BlockSpec((1, LANE), lambda b, h, i: (0, 0)),
                pl.BlockSpec((tq, LANE), lambda b, h, i: (qbase + b * nq + i, h)),
                pl.BlockSpec((t_seq, LANE), lambda b, h, i: (kbase + b, C_HEADS + h)),
                pl.BlockSpec((t_seq, LANE), lambda b, h, i: (kbase + b, 2 * C_HEADS + h))]
    args = [lamv, c_on.reshape(1, C_DV), qk, qk, p]
    if cache is not None:
        ck, cv, layer = cache
        lc = ck.shape[2]
        in_specs += [pl.BlockSpec((None, None, lc, LANE), lambda b, h, i: (b, layer, 0, h)),
                     pl.BlockSpec((None, None, lc, LANE), lambda b, h, i: (b, layer, 0, h))]
        args += [ck.reshape(ck.shape[:3] + (C_HEADS * LANE,)), cv.reshape(cv.shape[:3] + (C_HEADS * LANE,))]
    return pl.pallas_call(
        functools.partial(_diff_attn_body, lam_init=lam_init, has_cache=cache is not None),
        grid=(n_seq, C_HEADS, nq),
        in_specs=in_specs,
        out_specs=pl.BlockSpec((tq, LANE), lambda b, h, i: (b * nq + i, h)),
        out_shape=jax.ShapeDtypeStruct((n_seq * t_seq, C_HEADS * C_DV), F32),
        compiler_params=_params(3),
        name="diff_attn",
    )(*args)


def _top_rows(s, k, out_ref, want_rank=False):
    cur = s
    first = None
    rank = jnp.full(s.shape, float(k), F32) if want_rank else None
    for r in range(k):
        m = jnp.max(cur, axis=0, keepdims=True)
        out_ref[r:r + 1, :] = m
        first = m if first is None else first
        hit = cur == m
        if want_rank:
            rank = jnp.where(hit, float(r), rank)
        cur = jnp.where(hit, NEG_BIG, cur)
    return first, m, rank


_CAND_COUNTS = tuple(P_TOPK // (a + 1) for a in range(P_TOPK))
_CAND_ROWS = SUBLANES * ((sum(_CAND_COUNTS) + SUBLANES - 1) // SUBLANES)


def _peer_select(x_ref, g_ref, sc_ref, sh_ref, wq_ref, sk_ref,
                 h_s, q_s, r0_s, e0_s, n1_s, e1_s, top0_s, top1_s, cand_s):
    h = _rms_mod(x_ref[...], g_ref[...], sc_ref[0], sh_ref[0])
    h_s[...] = h.T.astype(BF16)
    q_s[...] = _dot(h, wq_ref[...])

    def head(hd, carry):
        qp = q_s[:, pl.ds(pl.multiple_of(hd * LANE, LANE), LANE)]
        s_t = lax.dot_general(sk_ref[hd], qp, (((1,), (1,)), ((), ())), precision=HI,
                              preferred_element_type=F32)
        for t in range(s_t.shape[1] // LANE):
            cols = slice(t * LANE, (t + 1) * LANE)
            s0 = s_t[:P_NKEYS, cols]
            s1 = s_t[P_NKEYS:, cols]
            max0, _, r0 = _top_rows(s0, P_TOPK, top0_s, want_rank=True)
            max1, _, _ = _top_rows(s1, P_TOPK, top1_s)
            row = 0
            for a, cnt in enumerate(_CAND_COUNTS):
                cand_s[row:row + cnt, :] = top0_s[a:a + 1, :] + top1_s[0:cnt, :]
                row += cnt
            cand_s[row:, :] = jnp.full((_CAND_ROWS - row, LANE), NEG_BIG, F32)
            cand = cand_s[...]
            _, thr, _ = _top_rows(cand, P_TOPK, top1_s)
            mx = max0 + max1
            z = jnp.sum(jnp.where(cand >= thr, jnp.exp(cand - mx), 0.0), axis=0, keepdims=True)
            n1 = jnp.zeros(s1.shape, F32)
            for a in range(P_TOPK // 2):
                n1 = n1 + jnp.where((top0_s[a:a + 1, :] + s1) >= thr, 1.0, 0.0)
            n_hi = jnp.zeros(max1.shape, F32)
            for a in range(P_TOPK // 2, P_TOPK):
                n_hi = n_hi + jnp.where((top0_s[a:a + 1, :] + max1) >= thr, 1.0, 0.0)
            n1 = n1 + jnp.where(s1 == max1, n_hi, 0.0)
            r0_s[hd, :, cols] = r0
            e0_s[hd, :, cols] = jnp.exp(s0 - max0)
            n1_s[hd, t] = n1.astype(BF16)
            e1_s[hd, t] = (jnp.exp(s1 - max1) / z).astype(BF16)
        return carry

    lax.fori_loop(0, P_HEADS, head, 0)


def _bf16_pair_words(x):
    bits = lax.bitcast_convert_type(x.astype(BF16).astype(F32), jnp.uint32)
    return bits | (bits >> 16)


def _packed_row(w):
    return pltpu.bitcast(jnp.broadcast_to(w, (GATE_ROWS // 2, LANE)), BF16)


def _peer_step(group, u_ref, vt_ref, h_s, r0_s, e0_s, r0c_s, e0c_s, n1_s, e1_s,
               st_score, st_gate, wt_gate, wt_acc, acc_s, *, do_acc, do_gate, do_score):
    tn = h_s.shape[1]
    n_t = tn // LANE
    n_pieces = n_t * (P_NKEYS // GATE_ROWS) // 2
    assert n_pieces * MXU_PIECE_ROWS == u_ref.shape[0] == vt_ref.shape[0]
    if do_gate:
        r0c_s[...] = _bf16_pair_words(r0_s[:, group, :])
        e0c_s[...] = _bf16_pair_words(e0_s[:, group, :])
    zero = jnp.zeros((GATE_ROWS, LANE), BF16)
    for t in range(n_t):
        cols = slice(t * LANE, (t + 1) * LANE)
        for jq in range(P_NKEYS // GATE_ROWS):
            jr = slice(jq * GATE_ROWS, (jq + 1) * GATE_ROWS)
            g = [zero for _ in range(SUBLANES)]
            for hd in range(P_HEADS if do_gate else 0):
                n1q = n1_s[hd, t, jr, :]
                e1q = e1_s[hd, t, jr, :]
                for il in range(SUBLANES):
                    r0r = _packed_row(r0c_s[hd, il:il + 1, cols])
                    e0r = _packed_row(e0c_s[hd, il:il + 1, cols])
                    g[il] = g[il] + e0r * jnp.where(r0r < n1q, e1q, zero)
            for il in range(SUBLANES if do_gate else 0):
                rows = slice(il * P_NKEYS + jq * GATE_ROWS, il * P_NKEYS + (jq + 1) * GATE_ROWS)
                sc = st_gate[t, rows, :].astype(BF16)
                act = (0.5 * sc) * (1.0 + lax.erf(sc * (2.0 ** -0.5)))
                wt_gate[rows, cols] = g[il] * act
            piece = (t * (P_NKEYS // GATE_ROWS) + jq) // 2
            ms = slice(piece * MXU_PIECE_ROWS, (piece + 1) * MXU_PIECE_ROWS)
            if jq % 2 == 0 and do_acc:
                acc_s[ms, :] += _dot(vt_ref[ms, :], wt_acc[...])
            if jq % 2 == 1 and do_score:
                s_new = _dot(u_ref[ms, :], h_s[...])
                for tt in range(n_t):
                    st_score[tt, ms, :] = s_new[:, tt * LANE:(tt + 1) * LANE]


def _peer_body(x_ref, g_ref, sc_ref, sh_ref, gate_ref, wq_ref, sk_ref, u_ref, vt_ref, o_ref,
               h_s, q_s, r0_s, e0_s, r0c_s, e0c_s, n1_s, e1_s, top0_s, top1_s, cand_s,
               st0_s, st1_s, wt0_s, wt1_s, acc_s, *, n_eb):
    e = pl.program_id(1)
    assert u_ref.shape[0] == SUBLANES * P_NKEYS
    assert n_eb >= 2

    @pl.when(e == 0)
    def _():
        _peer_select(x_ref, g_ref, sc_ref, sh_ref, wq_ref, sk_ref,
                     h_s, q_s, r0_s, e0_s, n1_s, e1_s, top0_s, top1_s, cand_s)
        acc_s[...] = jnp.zeros_like(acc_s)

    gate_block = jnp.clip(e - 1, 0, n_eb - 1)
    group = pl.ds(pl.multiple_of(gate_block * SUBLANES, SUBLANES), SUBLANES)
    st = (st0_s, st1_s)
    wt = (wt0_s, wt1_s)

    def step(cond, slot, **parts):
        @pl.when(cond)
        def _():
            _peer_step(group, u_ref, vt_ref, h_s, r0_s, e0_s, r0c_s, e0c_s, n1_s, e1_s,
                       st[slot], st[1 - slot], wt[1 - slot], wt[slot], acc_s, **parts)

    step(e == 0, 0, do_acc=False, do_gate=False, do_score=True)
    step(e == 1, 1, do_acc=False, do_gate=True, do_score=True)
    for slot in range(2):
        step((e >= 2) & (e < n_eb) & (e % 2 == slot), slot, do_acc=True, do_gate=True, do_score=True)
    step(e == n_eb, n_eb % 2, do_acc=True, do_gate=True, do_score=False)
    step(e == n_eb + 1, (n_eb + 1) % 2, do_acc=True, do_gate=False, do_score=False)

    @pl.when(e == n_eb + 1)
    def _():
        o_ref[...] = x_ref[...] + gate_ref[0] * acc_s[...].T


def _peer(lay, x, g, mod, wq, sk2, u, vt):
    tn, en = PEER_TOKENS, PEER_EXPERTS
    n_eb = u.shape[0] // en
    grid = (lay.nt // tn, n_eb + 2)
    head_scr = pltpu.VMEM((P_HEADS, P_NKEYS, tn), F32)
    rows_scr = pltpu.VMEM((P_HEADS, SUBLANES, tn), jnp.uint32)
    tile_scr = pltpu.VMEM((P_HEADS, tn // LANE, P_NKEYS, LANE), BF16)
    score_scr = pltpu.VMEM((tn // LANE, en, LANE), F32)
    gated_scr = pltpu.VMEM((en, tn), BF16)
    return pl.pallas_call(
        functools.partial(_peer_body, n_eb=n_eb),
        grid=grid,
        in_specs=[pl.BlockSpec((tn, D_MODEL), lambda i, e: (i, 0)),
                  pl.BlockSpec((1, D_MODEL), lambda i, e: (0, 0)),
                  _mod_spec(lay, tn, 4), _mod_spec(lay, tn, 3), _mod_spec(lay, tn, 5),
                  pl.BlockSpec((D_MODEL, P_HEADS * LANE), lambda i, e: (0, 0)),
                  pl.BlockSpec((P_HEADS, 2 * P_NKEYS, LANE), lambda i, e: (0, 0, 0)),
                  pl.BlockSpec((en, D_MODEL), lambda i, e: (jnp.minimum(e, n_eb - 1), 0)),
                  pl.BlockSpec((D_MODEL, en), lambda i, e: (0, jnp.clip(e - 2, 0, n_eb - 1)))],
        out_specs=pl.BlockSpec((tn, D_MODEL), lambda i, e: (i, 0)),
        out_shape=jax.ShapeDtypeStruct((lay.nt, D_MODEL), F32),
        scratch_shapes=[pltpu.VMEM((D_MODEL, tn), BF16),
                        pltpu.VMEM((tn, P_HEADS * LANE), F32),
                        head_scr, head_scr, rows_scr, rows_scr, tile_scr, tile_scr,
                        pltpu.VMEM((P_TOPK, LANE), F32),
                        pltpu.VMEM((P_TOPK, LANE), F32),
                        pltpu.VMEM((_CAND_ROWS, LANE), F32),
                        score_scr, score_scr, gated_scr, gated_scr,
                        pltpu.VMEM((D_MODEL, tn), F32)],
        compiler_params=_params(2),
        name="peer",
    )(x, g.reshape(1, D_MODEL), mod, mod, mod, wq, sk2, u, vt)


def _sub_key_blocks(sub_keys):
    z = jnp.zeros_like(sub_keys[:, 0])
    top = jnp.concatenate([sub_keys[:, 0], z], axis=-1)
    bot = jnp.concatenate([z, sub_keys[:, 1]], axis=-1)
    return jnp.concatenate([top, bot], axis=1)


def _lambda_init(layer):
    return 0.8 - 0.6 * math.exp(-0.3 * layer)


def _even_w_in(w):
    aq, ak, av, bq, bk, bv, br, bg = jnp.split(w, [512, 640, 768, 1024, 1280, 1792, 2304], axis=1)
    pad = jnp.zeros((w.shape[0], E_WIDTH - E_BG - bg.shape[1]), w.dtype)
    return jnp.concatenate([aq, ak, av, bq, bv, br, bk, bg, pad], axis=1).astype(BF16)


def kernel(x_prompt, x_sample, cache_a_k, cache_a_v, state_b_fwd, state_b_bwd, cache_c_k, cache_c_v, c, c_ctx, ada_w, ada_b, norm_mix_g, norm_ffn_g, e_w_in, e_w_out, a_q_norm, a_k_norm, a_sink, b_gate_w_f, b_gate_b_f, b_gate_w_b, b_gate_b_b, b_out_norm, o_w_in, o_w_out, c_q_norm, c_k_norm, c_lambda_q1, c_lambda_k1, c_lambda_q2, c_lambda_k2, c_out_norm, p_w_q, p_sub_keys, p_u, p_v):
    bc, tc, d = x_prompt.shape
    bl, tl, _ = x_sample.shape
    depth = ada_w.shape[0]
    lay = _Layout(bc, tc, bl, tl)
    n_mod_rows = 8 * ((bl + 1 + 7) // 8)
    cvecs = jnp.concatenate([c, c_ctx[None, :], jnp.zeros((n_mod_rows - bl - 1, d), F32)], axis=0)
    mods = _modulation(cvecs, ada_w, ada_b)
    cos, sin = _rope_tables(tl)
    x = jnp.concatenate([x_prompt.reshape(lay.nctx, d), x_sample.reshape(bl * tl, d)], axis=0)

    ak_l, av_l, sf_l, sb_l, ck_l, cv_l = [], [], [], [], [], []
    for l in range(depth):
        i = l // 2
        mod = mods[l].reshape(n_mod_rows, 1, N_MOD * d)
        if l % 2 == 0:
            p = _ln_proj(lay, x, norm_mix_g[l], mod, 0, _even_w_in(e_w_in[i]))
            gains = jnp.concatenate([jnp.tile(a_q_norm[i], A_HEADS), jnp.tile(a_k_norm[i], A_KV_HEADS)])
            qk = _qk_prep(lay, p, E_QK_WIDTH, gains.reshape(1, E_QK_WIDTH), cos, sin)
            sink = a_sink[i].reshape(A_HEADS)
            oa = jnp.concatenate([_attn_a_ctx(lay, qk, p, sink),
                                  _attn_a_lat(lay, qk, p, sink, cache_a_k, cache_a_v, i)], axis=0)
            zrow = jnp.zeros((LANE - 2 * B_GATE_RANK, 256), F32)
            zgate = jnp.zeros((B_GATE_RANK, 256), F32)
            gw_f = jnp.concatenate([b_gate_w_f[i], zgate, zrow], axis=0)
            gw_b = jnp.concatenate([zgate, b_gate_w_b[i], zrow], axis=0)
            o_f, s_f = _gla(lay, p, gw_f, b_gate_b_f[i], _state_in(state_b_fwd[:, i]), False)
            ob, s_b = _gla(lay, p, gw_b, b_gate_b_b[i], _state_in(state_b_bwd[:, i]), True,
                           o_fwd=o_f, b_on=b_out_norm[i])
            w_out = e_w_out[i].astype(BF16)
            x = _linear_res(lay, [oa, ob], [w_out[:512], w_out[512:]], mod, 2, x)
            ak_l.append(qk[:lay.nctx, E_AK:E_AK + LANE].reshape(bc, tc, A_KV_HEADS, HEAD_DIM))
            av_l.append(p[:lay.nctx, E_AV:E_AV + LANE].reshape(bc, tc, A_KV_HEADS, HEAD_DIM))
            sf_l.append(_state_out(s_f[:bc]))
            sb_l.append(_state_out(s_b[:bc]))
        else:
            lam_init = _lambda_init(l)
            p = _ln_proj(lay, x, norm_mix_g[l], mod, 0, o_w_in[i].astype(BF16))
            gains = jnp.concatenate([jnp.tile(c_q_norm[i], 2 * C_HEADS), jnp.tile(c_k_norm[i], 2 * C_HEADS)])
            qk = _qk_prep(lay, p, O_QK_WIDTH, gains.reshape(1, O_QK_WIDTH), cos, sin)
            lamv = jnp.stack([c_lambda_q1[i], c_lambda_k1[i], c_lambda_q2[i], c_lambda_k2[i]])
            lamv = jnp.pad(lamv, ((0, 4), (0, LANE - HEAD_DIM)))
            o_ctx = _diff_attn(qk, p, lamv, c_out_norm[i], lam_init, bc, tc, 0, tc)
            o_lat = _diff_attn(qk, p, lamv, c_out_norm[i], lam_init, bl, tl, lay.nctx, DIFF_Q_BLOCK,
                               cache=(cache_c_k, cache_c_v, i))
            o = jnp.concatenate([o_ctx, o_lat], axis=0)
            x = _linear_res(lay, [o], [o_w_out[i].astype(BF16)], mod, 2, x)
            ck_l.append(qk[:lay.nctx, 1024:2048].reshape(bc, tc, C_HEADS, 2, HEAD_DIM))
            cv_l.append(p[:lay.nctx, 2048:3072].reshape(bc, tc, C_HEADS, C_DV))
        x = _peer(lay, x, norm_ffn_g[l], mod, p_w_q[l].astype(BF16), _sub_key_blocks(p_sub_keys[l]),
                  p_u[l].astype(BF16), p_v[l].T.astype(BF16))

    return (x[:lay.nctx].reshape(bc, tc, d), x[lay.nctx:].reshape(bl, tl, d),
            jnp.stack(ak_l, axis=1), jnp.stack(av_l, axis=1), jnp.stack(sf_l, axis=1), jnp.stack(sb_l, axis=1),
            jnp.stack(ck_l, axis=1), jnp.stack(cv_l, axis=1))
```

```python
import functools
import math

import jax
import jax.numpy as jnp
from jax import lax
from jax.experimental import pallas as pl
from jax.experimental.pallas import tpu as pltpu

F32 = jnp.float32
BF16 = jnp.bfloat16
HI = lax.Precision.HIGHEST

D_MODEL = 1024
GRID_W = 64
HEAD_DIM = 64
ATTN_SCALE = HEAD_DIM ** -0.5
ROPE_BASE = 10000.0
RMS_EPS = 1e-6
N_MOD = 6
A_HEADS = 8
A_KV_HEADS = 2
A_WINDOW = 128
B_HEADS = 4
B_DK = 64
B_DV = 128
B_GATE_RANK = 16
B_GATE_TAU = 16.0
B_CHUNK = 64
C_HEADS = 8
C_DV = 128
P_HEADS = 8
P_NKEYS = 128
P_TOPK = 16

LANE = 128
SUBLANES = 8
ROW_BLOCK = 256
LIN_BLOCK = 512
PEER_TOKENS = 512
PEER_EXPERTS = SUBLANES * P_NKEYS
MXU_PIECE_ROWS = 128
GATE_ROWS = 32
ATTN_Q_BLOCK = 128
DIFF_Q_BLOCK = 256
NEG_BIG = -1e30
VMEM_LIMIT = 56 * 1024 * 1024

E_AQ, E_AK, E_AV, E_BQ, E_BV, E_BR, E_BK, E_BG, E_WIDTH = 0, 512, 640, 768, 1024, 1536, 2048, 2304, 2432
E_QK_WIDTH = 640
O_QK_WIDTH = 2048


def _dot(a, b):
    return lax.dot_general(a.astype(BF16), b.astype(BF16), (((1,), (0,)), ((), ())),
                           preferred_element_type=F32)


def _dot_nt(a, b):
    return lax.dot_general(a.astype(BF16), b.astype(BF16), (((1,), (1,)), ((), ())),
                           preferred_element_type=F32)


def _dot_hi(a, b):
    return lax.dot_general(a, b, (((1,), (0,)), ((), ())), precision=HI, preferred_element_type=F32)


def _params(n_axes, flags=None):
    return pltpu.CompilerParams(dimension_semantics=("arbitrary",) * n_axes,
                                vmem_limit_bytes=VMEM_LIMIT, flags=flags)


def _lo_lanes(shape):
    return (lax.broadcasted_iota(jnp.int32, shape, len(shape) - 1) & 64) == 0


def _mod_body(c_ref, w_ref, b_ref, o_ref):
    c = c_ref[...]
    s = c * jax.nn.sigmoid(c)
    o_ref[0] = _dot_hi(s, w_ref[0]) + b_ref[0]


def _modulation(cvecs, ada_w, ada_b):
    depth, d, n = ada_w.shape
    r = cvecs.shape[0]
    tn = 1536
    return pl.pallas_call(
        _mod_body,
        grid=(depth, n // tn),
        in_specs=[pl.BlockSpec((r, d), lambda l, j: (0, 0)),
                  pl.BlockSpec((1, d, tn), lambda l, j: (l, 0, j)),
                  pl.BlockSpec((1, 1, tn), lambda l, j: (l, 0, j))],
        out_specs=pl.BlockSpec((1, r, tn), lambda l, j: (l, 0, j)),
        out_shape=jax.ShapeDtypeStruct((depth, r, n), F32),
        compiler_params=_params(2),
        name="modulation",
    )(cvecs, ada_w, ada_b.reshape(depth, 1, n))


class _Layout:
    def __init__(self, bc, tc, bl, tl):
        self.bc, self.tc, self.bl, self.tl = bc, tc, bl, tl
        self.nctx = bc * tc
        self.nt = self.nctx + bl * tl
        assert tc == ROW_BLOCK and tl % LIN_BLOCK == 0 and self.nctx % LIN_BLOCK == 0
        assert self.nctx % tl == 0 and tl % GRID_W == 0 and self.nt % PEER_TOKENS == 0
        assert self.nctx % PEER_TOKENS == 0 and tl % PEER_TOKENS == 0

    def mod_row(self, i, block):
        nctx_blocks = self.nctx // block
        return jnp.where(i < nctx_blocks, self.bl, (i - nctx_blocks) // (self.tl // block))


def _mod_spec(lay, block, chunk):
    return pl.BlockSpec((1, 1, D_MODEL), lambda i, *_: (lay.mod_row(i, block), 0, chunk))


def _rms_mod(x, g, sc, sh):
    ms = jnp.mean(x * x, axis=-1, keepdims=True)
    return (x * lax.rsqrt(ms + RMS_EPS) * g) * (1.0 + sc) + sh


def _ln_proj_body(x_ref, g_ref, sc_ref, sh_ref, w_ref, o_ref):
    h = _rms_mod(x_ref[...], g_ref[...], sc_ref[0], sh_ref[0])
    o_ref[...] = _dot(h, w_ref[...])


def _ln_proj(lay, x, g, mod, chunk, w):
    n_out = w.shape[1]
    return pl.pallas_call(
        _ln_proj_body,
        grid=(lay.nt // LIN_BLOCK,),
        in_specs=[pl.BlockSpec((LIN_BLOCK, D_MODEL), lambda i: (i, 0)),
                  pl.BlockSpec((1, D_MODEL), lambda i: (0, 0)),
                  _mod_spec(lay, LIN_BLOCK, chunk + 1),
                  _mod_spec(lay, LIN_BLOCK, chunk),
                  pl.BlockSpec((D_MODEL, n_out), lambda i: (0, 0))],
        out_specs=pl.BlockSpec((LIN_BLOCK, n_out), lambda i: (i, 0)),
        out_shape=jax.ShapeDtypeStruct((lay.nt, n_out), F32),
        compiler_params=_params(1),
        name="ln_proj",
    )(x, g.reshape(1, D_MODEL), mod, mod, w)


def _prep_body(x_ref, g_ref, cos_ref, sin_ref, bd_ref, o_ref, *, n_chunks):
    c = cos_ref[...]
    s = sin_ref[...]
    first = (lax.broadcasted_iota(jnp.int32, c.shape, 1) & 16) == 0
    for j in range(n_chunks):
        cols = slice(j * LANE, (j + 1) * LANE)
        x = x_ref[:, cols]
        ms = _dot_hi(x * x, bd_ref[...])
        y = x * lax.rsqrt(ms + RMS_EPS) * g_ref[:, cols]
        partner = jnp.where(first, pltpu.roll(y, LANE - 16, 1), pltpu.roll(y, 16, 1))
        o_ref[:, cols] = y * c + partner * s


def _rope_tables(tl):
    t = jnp.arange(tl)
    row = (t // GRID_W).astype(F32)
    col = (t % GRID_W).astype(F32)
    nf = HEAD_DIM // 4
    freqs = ROPE_BASE ** (-jnp.arange(nf, dtype=F32) / nf)
    ang_r = row[:, None] * freqs
    ang_c = col[:, None] * freqs
    cos64 = jnp.concatenate([jnp.cos(ang_r), jnp.cos(ang_r), jnp.cos(ang_c), jnp.cos(ang_c)], axis=1)
    sin64 = jnp.concatenate([-jnp.sin(ang_r), jnp.sin(ang_r), -jnp.sin(ang_c), jnp.sin(ang_c)], axis=1)
    cos = jnp.concatenate([jnp.ones((ROW_BLOCK, HEAD_DIM), F32), cos64], axis=0)
    sin = jnp.concatenate([jnp.zeros((ROW_BLOCK, HEAD_DIM), F32), sin64], axis=0)
    return jnp.tile(cos, (1, 2)), jnp.tile(sin, (1, 2))


def _qk_prep(lay, p, width, gains, cos, sin):
    n_ctx_blocks = lay.nctx // ROW_BLOCK
    per_seq = lay.tl // ROW_BLOCK

    def tab(i):
        return (jnp.where(i < n_ctx_blocks, 0, 1 + (i - n_ctx_blocks) % per_seq), 0)

    seg = jnp.arange(LANE) // HEAD_DIM
    bd = (seg[:, None] == seg[None, :]).astype(F32) / HEAD_DIM
    return pl.pallas_call(
        functools.partial(_prep_body, n_chunks=width // LANE),
        grid=(lay.nt // ROW_BLOCK,),
        in_specs=[pl.BlockSpec((ROW_BLOCK, width), lambda i: (i, 0)),
                  pl.BlockSpec((1, width), lambda i: (0, 0)),
                  pl.BlockSpec((ROW_BLOCK, LANE), tab),
                  pl.BlockSpec((ROW_BLOCK, LANE), tab),
                  pl.BlockSpec((LANE, LANE), lambda i: (0, 0))],
        out_specs=pl.BlockSpec((ROW_BLOCK, width), lambda i: (i, 0)),
        out_shape=jax.ShapeDtypeStruct((lay.nt, width), F32),
        compiler_params=_params(1),
        name="qk_prep",
    )(p, gains, cos, sin, bd)


def _dup_kv(x):
    lo = _lo_lanes(x.shape)
    xr = pltpu.roll(x, HEAD_DIM, 1)
    return jnp.where(lo, x, xr), jnp.where(lo, xr, x)


def _gqa_heads(sink_ref, q_ref, k2, v2, bias, o_ref):
    rows = q_ref.shape[0]
    lo = _lo_lanes((rows, LANE))
    for pair in range(A_HEADS // 2):
        kv = pair // 2
        cols = slice(pair * LANE, (pair + 1) * LANE)
        qp = q_ref[:, cols] * ATTN_SCALE
        outs = []
        for half in range(2):
            qm = jnp.where(lo, qp, 0.0) if half == 0 else jnp.where(lo, 0.0, qp)
            s = _dot_nt(qm, k2[kv])
            if bias is not None:
                s = s + bias
            sk = sink_ref[2 * pair + half]
            mx = jnp.maximum(jnp.max(s, axis=-1, keepdims=True), sk)
            p = jnp.exp(s - mx)
            den = jnp.sum(p, axis=-1, keepdims=True) + jnp.exp(sk - mx)
            outs.append(_dot(p, v2[kv]) / den)
        o_ref[:, cols] = jnp.where(lo, outs[0], outs[1])


def _attn_a_ctx_body(sink_ref, q_ref, k_ref, v_ref, o_ref):
    _gqa_heads(sink_ref, q_ref, _dup_kv(k_ref[...]), _dup_kv(v_ref[...]), None, o_ref)


def _attn_a_ctx(lay, qk, p, sink):
    return pl.pallas_call(
        _attn_a_ctx_body,
        grid=(lay.bc,),
        in_specs=[pl.BlockSpec(memory_space=pltpu.SMEM),
                  pl.BlockSpec((lay.tc, 512), lambda b: (b, 0)),
                  pl.BlockSpec((lay.tc, LANE), lambda b: (b, E_AK // LANE)),
                  pl.BlockSpec((lay.tc, LANE), lambda b: (b, E_AV // LANE))],
        out_specs=pl.BlockSpec((lay.tc, 512), lambda b: (b, 0)),
        out_shape=jax.ShapeDtypeStruct((lay.nctx, 512), F32),
        compiler_params=_params(1),
        name="attn_a_ctx",
    )(sink, qk, qk, p)


def _attn_a_lat_body(sink_ref, q_ref, k0, k1, k2r, v0, v1, v2r, kc_ref, vc_ref, o_ref):
    n = pl.program_id(1)
    last = pl.num_programs(1) - 1
    w = ATTN_Q_BLOCK
    lc = kc_ref.shape[0]
    k_all = jnp.concatenate([k0[...], k1[...], k2r[...], kc_ref[...]], axis=0)
    v_all = jnp.concatenate([v0[...], v1[...], v2r[...], vc_ref[...]], axis=0)
    i = lax.broadcasted_iota(jnp.int32, (w, 3 * w + lc), 0)
    j = lax.broadcasted_iota(jnp.int32, (w, 3 * w + lc), 1)
    valid = (jnp.abs(i + w - j) <= A_WINDOW) & ((j >= w) | (n > 0)) & ((j < 2 * w) | (n < last))
    bias = jnp.where(valid | (j >= 3 * w), 0.0, NEG_BIG)
    _gqa_heads(sink_ref, q_ref, _dup_kv(k_all), _dup_kv(v_all), bias, o_ref)


def _attn_a_lat(lay, qk, p, sink, cache_k, cache_v, layer):
    w = ATTN_Q_BLOCK
    nb = lay.tl // w
    base = lay.nctx // w
    lc = cache_k.shape[2]

    def row(off):
        return lambda b, n: base + b * nb + jnp.clip(n + off, 0, nb - 1)

    def kspec(off):
        r = row(off)
        return pl.BlockSpec((w, LANE), lambda b, n: (r(b, n), E_AK // LANE))

    def vspec(off):
        r = row(off)
        return pl.BlockSpec((w, LANE), lambda b, n: (r(b, n), E_AV // LANE))

    cspec = pl.BlockSpec((None, None, lc, LANE), lambda b, n: (b, layer, 0, 0))
    r0 = row(0)
    return pl.pallas_call(
        _attn_a_lat_body,
        grid=(lay.bl, nb),
        in_specs=[pl.BlockSpec(memory_space=pltpu.SMEM),
                  pl.BlockSpec((w, 512), lambda b, n: (r0(b, n), 0)),
                  kspec(-1), kspec(0), kspec(1), vspec(-1), vspec(0), vspec(1), cspec, cspec],
        out_specs=pl.BlockSpec((w, 512), lambda b, n: (b * nb + n, 0)),
        out_shape=jax.ShapeDtypeStruct((lay.bl * lay.tl, 512), F32),
        compiler_params=_params(2),
        name="attn_a_lat",
    )(sink, qk, qk, qk, qk, p, p, p,
      cache_k.reshape(cache_k.shape[:3] + (LANE,)), cache_v.reshape(cache_v.shape[:3] + (LANE,)))


def _gla_body(*refs, reverse, n_ctx_blocks, per_seq):
    if reverse:
        (bq_ref, bk_ref, bv_ref, bg_ref, gw_ref, gb_ref, s0_ref, tri_ref, of_ref, br_ref, bon_ref,
         o_ref, sfin_ref, st) = refs
    else:
        bq_ref, bk_ref, bv_ref, bg_ref, gw_ref, gb_ref, s0_ref, tri_ref, o_ref, sfin_ref, st = refs
    i = pl.program_id(0)
    is_start = (i < n_ctx_blocks) | (((i - n_ctx_blocks) % per_seq) == 0)

    @pl.when(is_start)
    def _():
        st[...] = s0_ref[...]

    z = _dot_hi(bg_ref[...], gw_ref[...]) + gb_ref[...]
    lg = (jnp.minimum(z, 0.0) - jnp.log1p(jnp.exp(-jnp.abs(z)))) * (1.0 / B_GATE_TAU)
    q = bq_ref[...] * (B_DK ** -0.5)
    k = bk_ref[...]
    tri = tri_ref[...]
    L = B_CHUNK
    n_chunks = ROW_BLOCK // L
    lo = _lo_lanes((L, LANE))
    ri = lax.broadcasted_iota(jnp.int32, (L, L), 0)
    ci = lax.broadcasted_iota(jnp.int32, (L, L), 1)
    causal = (ri <= ci) if reverse else (ri >= ci)
    order = range(n_chunks - 1, -1, -1) if reverse else range(n_chunks)
    for c in order:
        rows = slice(c * L, (c + 1) * L)
        b = _dot_hi(tri, lg[rows])
        b_last = b[0:1] if reverse else b[L - 1:L]
        qd = q[rows] * jnp.exp(b)
        kd = k[rows] * jnp.exp(-b)
        kt = k[rows] * jnp.exp(b_last - b)
        dec = jnp.exp(b_last)
        for h in range(B_HEADS):
            cols = slice((h // 2) * LANE, (h // 2 + 1) * LANE)
            mine = lo if h % 2 == 0 else jnp.logical_not(lo)
            qm = jnp.where(mine, qd[:, cols], 0.0)
            a = jnp.where(causal, _dot_nt(qm, kd[:, cols]), 0.0)
            vh = bv_ref[rows, h * B_DV:(h + 1) * B_DV]
            s_t = st[h]
            o = _dot(a, vh) + _dot_nt(qm, s_t)
            kv_t = _dot(vh.T, jnp.where(mine, kt[:, cols], 0.0))
            st[h] = s_t * dec[:, cols] + kv_t
            if reverse:
                tot = of_ref[rows, h * B_DV:(h + 1) * B_DV] + o
                ms = jnp.mean(tot * tot, axis=-1, keepdims=True)
                brh = br_ref[rows, h * B_DV:(h + 1) * B_DV]
                o = tot * lax.rsqrt(ms + RMS_EPS) * bon_ref[...] * (brh * jax.nn.sigmoid(brh))
            o_ref[rows, h * B_DV:(h + 1) * B_DV] = o
    sfin_ref[...] = st[...]


def _gla(lay, p, gw_pad, gb, s0, reverse, o_fwd=None, b_on=None):
    n_ctx_blocks = lay.nctx // ROW_BLOCK
    per_seq = lay.tl // ROW_BLOCK
    n_blocks = lay.nt // ROW_BLOCK

    def blk(i):
        j = i - n_ctx_blocks
        within = j % per_seq
        lat = n_ctx_blocks + (j - within) + (per_seq - 1 - within if reverse else within)
        return jnp.where(i < n_ctx_blocks, i, lat)

    def seq(i):
        return jnp.where(i < n_ctx_blocks, i, n_ctx_blocks + (i - n_ctx_blocks) // per_seq)

    def s0_idx(i):
        return jnp.where(i < n_ctx_blocks, lay.bl, (i - n_ctx_blocks) // per_seq)

    ri = jnp.arange(B_CHUNK)
    tri = ((ri[:, None] <= ri[None, :]) if reverse else (ri[:, None] >= ri[None, :])).astype(F32)
    in_specs = [pl.BlockSpec((ROW_BLOCK, 256), lambda i: (blk(i), E_BQ // 256)),
                pl.BlockSpec((ROW_BLOCK, 256), lambda i: (blk(i), E_BK // 256)),
                pl.BlockSpec((ROW_BLOCK, 512), lambda i: (blk(i), E_BV // 512)),
                pl.BlockSpec((ROW_BLOCK, LANE), lambda i: (blk(i), E_BG // LANE)),
                pl.BlockSpec((LANE, 256), lambda i: (0, 0)),
                pl.BlockSpec((1, 256), lambda i: (0, 0)),
                pl.BlockSpec((None, B_HEADS, LANE, LANE), lambda i: (s0_idx(i), 0, 0, 0)),
                pl.BlockSpec((B_CHUNK, B_CHUNK), lambda i: (0, 0))]
    args = [p, p, p, p, gw_pad, gb.reshape(1, 256), s0, tri]
    if reverse:
        in_specs += [pl.BlockSpec((ROW_BLOCK, 512), lambda i: (blk(i), 0)),
                     pl.BlockSpec((ROW_BLOCK, 512), lambda i: (blk(i), E_BR // 512)),
                     pl.BlockSpec((1, B_DV), lambda i: (0, 0))]
        args += [o_fwd, p, b_on.reshape(1, B_DV)]
    n_seq = n_ctx_blocks + lay.bl
    return pl.pallas_call(
        functools.partial(_gla_body, reverse=reverse, n_ctx_blocks=n_ctx_blocks, per_seq=per_seq),
        grid=(n_blocks,),
        in_specs=in_specs,
        out_specs=[pl.BlockSpec((ROW_BLOCK, 512), lambda i: (blk(i), 0)),
                   pl.BlockSpec((None, B_HEADS, LANE, LANE), lambda i: (seq(i), 0, 0, 0))],
        out_shape=[jax.ShapeDtypeStruct((lay.nt, 512), F32),
                   jax.ShapeDtypeStruct((n_seq, B_HEADS, LANE, LANE), F32)],
        scratch_shapes=[pltpu.VMEM((B_HEADS, LANE, LANE), F32)],
        compiler_params=_params(1),
        name="gla_bwd" if reverse else "gla_fwd",
    )(*args)


def _state_in(s):
    st = jnp.swapaxes(s.astype(F32), 2, 3)
    z = jnp.zeros_like(st)
    even = jnp.concatenate([st, z], axis=-1)
    odd = jnp.concatenate([z, st], axis=-1)
    pick = (jnp.arange(B_HEADS) % 2 == 0)[None, :, None, None]
    full = jnp.where(pick, even, odd)
    return jnp.concatenate([full, jnp.zeros_like(full[:1])], axis=0)


def _state_out(s_t):
    even = s_t[..., :B_DK]
    odd = s_t[..., B_DK:]
    pick = (jnp.arange(B_HEADS) % 2 == 0)[None, :, None, None]
    return jnp.swapaxes(jnp.where(pick, even, odd), 2, 3)


def _linear_res_body(*refs, n_in):
    a_refs, w_refs = refs[:n_in], refs[n_in:2 * n_in]
    gate_ref, res_ref, o_ref = refs[2 * n_in:]
    acc = _dot(a_refs[0][...], w_refs[0][...])
    for a_ref, w_ref in zip(a_refs[1:], w_refs[1:]):
        acc = acc + _dot(a_ref[...], w_ref[...])
    o_ref[...] = res_ref[...] + gate_ref[0] * acc


def _linear_res(lay, xs, ws, mod, chunk, res):
    n_in = len(xs)
    in_specs = [pl.BlockSpec((LIN_BLOCK, a.shape[1]), lambda i: (i, 0)) for a in xs]
    in_specs += [pl.BlockSpec(w.shape, lambda i: (0, 0)) for w in ws]
    in_specs += [_mod_spec(lay, LIN_BLOCK, chunk),
                 pl.BlockSpec((LIN_BLOCK, D_MODEL), lambda i: (i, 0))]
    return pl.pallas_call(
        functools.partial(_linear_res_body, n_in=n_in),
        grid=(lay.nt // LIN_BLOCK,),
        in_specs=in_specs,
        out_specs=pl.BlockSpec((LIN_BLOCK, D_MODEL), lambda i: (i, 0)),
        out_shape=jax.ShapeDtypeStruct((lay.nt, D_MODEL), F32),
        compiler_params=_params(1),
        name="linear_res",
    )(*xs, *ws, mod, res)


def _diff_attn_body(*refs, lam_init, has_cache):
    if has_cache:
        lamv_ref, con_ref, q_ref, k_ref, v_ref, kc_ref, vc_ref, o_ref = refs
    else:
        lamv_ref, con_ref, q_ref, k_ref, v_ref, o_ref = refs
    lv = lamv_ref[...]
    lam = (jnp.exp(jnp.sum(lv[0:1] * lv[1:2], axis=-1, keepdims=True))
           - jnp.exp(jnp.sum(lv[2:3] * lv[3:4], axis=-1, keepdims=True)) + lam_init)
    q = q_ref[...] * ATTN_SCALE
    tq = q.shape[0]
    lo = _lo_lanes(q.shape)
    qs = jnp.concatenate([jnp.where(lo, q, 0.0), jnp.where(lo, 0.0, q)], axis=0)
    s = _dot_nt(qs, k_ref[...])
    mx = jnp.max(s, axis=-1, keepdims=True)
    if has_cache:
        sc = _dot_nt(qs, kc_ref[...])
        mx = jnp.maximum(mx, jnp.max(sc, axis=-1, keepdims=True))
    p = jnp.exp(s - mx)
    den = jnp.sum(p, axis=-1, keepdims=True)
    if has_cache:
        pc = jnp.exp(sc - mx)
        den = den + jnp.sum(pc, axis=-1, keepdims=True)
    inv = 1.0 / den
    w0 = inv[:tq]
    w1 = lam * inv[tq:]
    o = _dot(p[:tq] * w0 - p[tq:] * w1, v_ref[...])
    if has_cache:
        o = o + _dot(pc[:tq] * w0 - pc[tq:] * w1, vc_ref[...])
    ms = jnp.mean(o * o, axis=-1, keepdims=True)
    o_ref[...] = o * lax.rsqrt(ms + RMS_EPS) * con_ref[...] * (1.0 - lam_init)


def _diff_attn(qk, p, lamv, c_on, lam_init, n_seq, t_seq, row_base, tq, cache=None):
    nq = t_seq // tq
    qbase = row_base // tq
    kbase = row_base // t_seq
    in_specs = [pl.BlockSpec((8, LANE), lambda b, h, i: (0, 0)),
                pl.BlockSpec((1, LANE), lambda b, h, i: (0, 0)),
                pl.BlockSpec((tq, LANE), lambda b, h, i: (qbase + b * nq + i, h)),
                pl.BlockSpec((t_seq, LANE), lambda b, h, i: (kbase + b, C_HEADS + h)),
                pl.BlockSpec((t_seq, LANE), lambda b, h, i: (kbase + b, 2 * C_HEADS + h))]
    args = [lamv, c_on.reshape(1, C_DV), qk, qk, p]
    if cache is not None:
        ck, cv, layer = cache
        lc = ck.shape[2]
        in_specs += [pl.BlockSpec((None, None, lc, LANE), lambda b, h, i: (b, layer, 0, h)),
                     pl.BlockSpec((None, None, lc, LANE), lambda b, h, i: (b, layer, 0, h))]
        args += [ck.reshape(ck.shape[:3] + (C_HEADS * LANE,)), cv.reshape(cv.shape[:3] + (C_HEADS * LANE,))]
    return pl.pallas_call(
        functools.partial(_diff_attn_body, lam_init=lam_init, has_cache=cache is not None),
        grid=(n_seq, C_HEADS, nq),
        in_specs=in_specs,
        out_specs=pl.BlockSpec((tq, LANE), lambda b, h, i: (b * nq + i, h)),
        out_shape=jax.ShapeDtypeStruct((n_seq * t_seq, C_HEADS * C_DV), F32),
        compiler_params=_params(3),
        name="diff_attn",
    )(*args)


def _top_rows(s, k, out_ref, want_rank=False):
    cur = s
    first = None
    rank = jnp.full(s.shape, float(k), F32) if want_rank else None
    for r in range(k):
        m = jnp.max(cur, axis=0, keepdims=True)
        out_ref[r:r + 1, :] = m
        first = m if first is None else first
        hit = cur == m
        if want_rank:
            rank = jnp.where(hit, float(r), rank)
        cur = jnp.where(hit, NEG_BIG, cur)
    return first, m, rank


_CAND_COUNTS = tuple(P_TOPK // (a + 1) for a in range(P_TOPK))
_CAND_ROWS = SUBLANES * ((sum(_CAND_COUNTS) + SUBLANES - 1) // SUBLANES)


def _peer_select(x_ref, g_ref, sc_ref, sh_ref, wq_ref, sk_ref,
                 h_s, q_s, r0_s, e0_s, n1_s, e1_s, top0_s, top1_s, cand_s):
    h = _rms_mod(x_ref[...], g_ref[...], sc_ref[0], sh_ref[0])
    h_s[...] = h.T.astype(BF16)
    q_s[...] = _dot(h, wq_ref[...])

    def head(hd, carry):
        qp = q_s[:, pl.ds(pl.multiple_of(hd * LANE, LANE), LANE)]
        s_t = lax.dot_general(sk_ref[hd], qp, (((1,), (1,)), ((), ())), precision=HI,
                              preferred_element_type=F32)
        for t in range(s_t.shape[1] // LANE):
            cols = slice(t * LANE, (t + 1) * LANE)
            s0 = s_t[:P_NKEYS, cols]
            s1 = s_t[P_NKEYS:, cols]
            max0, _, r0 = _top_rows(s0, P_TOPK, top0_s, want_rank=True)
            max1, _, _ = _top_rows(s1, P_TOPK, top1_s)
            row = 0
            for a, cnt in enumerate(_CAND_COUNTS):
                cand_s[row:row + cnt, :] = top0_s[a:a + 1, :] + top1_s[0:cnt, :]
                row += cnt
            cand_s[row:, :] = jnp.full((_CAND_ROWS - row, LANE), NEG_BIG, F32)
            cand = cand_s[...]
            _, thr, _ = _top_rows(cand, P_TOPK, top1_s)
            mx = max0 + max1
            z = jnp.sum(jnp.where(cand >= thr, jnp.exp(cand - mx), 0.0), axis=0, keepdims=True)
            n1 = jnp.zeros(s1.shape, F32)
            for a in range(P_TOPK // 2):
                n1 = n1 + jnp.where((top0_s[a:a + 1, :] + s1) >= thr, 1.0, 0.0)
            n_hi = jnp.zeros(max1.shape, F32)
            for a in range(P_TOPK // 2, P_TOPK):
                n_hi = n_hi + jnp.where((top0_s[a:a + 1, :] + max1) >= thr, 1.0, 0.0)
            n1 = n1 + jnp.where(s1 == max1, n_hi, 0.0)
            r0_s[hd, :, cols] = r0
            e0_s[hd, :, cols] = jnp.exp(s0 - max0)
            n1_s[hd, t] = n1.astype(BF16)
            e1_s[hd, t] = (jnp.exp(s1 - max1) / z).astype(BF16)
        return carry

    lax.fori_loop(0, P_HEADS, head, 0)


def _bf16_pair_words(x):
    bits = lax.bitcast_convert_type(x.astype(BF16).astype(F32), jnp.uint32)
    return bits | (bits >> 16)


def _packed_row(w):
    return pltpu.bitcast(jnp.broadcast_to(w, (GATE_ROWS // 2, LANE)), BF16)


def _peer_step(group, u_ref, vt_ref, h_s, r0_s, e0_s, r0c_s, e0c_s, n1_s, e1_s,
               st_score, st_gate, wt_gate, wt_acc, acc_s, *, do_acc, do_gate, do_score):
    tn = h_s.shape[1]
    n_t = tn // LANE
    n_pieces = n_t * (P_NKEYS // GATE_ROWS) // 2
    assert n_pieces * MXU_PIECE_ROWS == u_ref.shape[0] == vt_ref.shape[0]
    if do_gate:
        r0c_s[...] = _bf16_pair_words(r0_s[:, group, :])
        e0c_s[...] = _bf16_pair_words(e0_s[:, group, :])
    zero = jnp.zeros((GATE_ROWS, LANE), BF16)
    for t in range(n_t):
        cols = slice(t * LANE, (t + 1) * LANE)
        for jq in range(P_NKEYS // GATE_ROWS):
            jr = slice(jq * GATE_ROWS, (jq + 1) * GATE_ROWS)
            g = [zero for _ in range(SUBLANES)]
            for hd in range(P_HEADS if do_gate else 0):
                n1q = n1_s[hd, t, jr, :]
                e1q = e1_s[hd, t, jr, :]
                for il in range(SUBLANES):
                    r0r = _packed_row(r0c_s[hd, il:il + 1, cols])
                    e0r = _packed_row(e0c_s[hd, il:il + 1, cols])
                    g[il] = g[il] + e0r * jnp.where(r0r < n1q, e1q, zero)
            for il in range(SUBLANES if do_gate else 0):
                rows = slice(il * P_NKEYS + jq * GATE_ROWS, il * P_NKEYS + (jq + 1) * GATE_ROWS)
                sc = st_gate[t, rows, :].astype(BF16)
                act = (0.5 * sc) * (1.0 + lax.erf(sc * (2.0 ** -0.5)))
                wt_gate[rows, cols] = g[il] * act
            piece = (t * (P_NKEYS // GATE_ROWS) + jq) // 2
            ms = slice(piece * MXU_PIECE_ROWS, (piece + 1) * MXU_PIECE_ROWS)
            if jq % 2 == 0 and do_acc:
                acc_s[ms, :] += _dot(vt_ref[ms, :], wt_acc[...])
            if jq % 2 == 1 and do_score:
                s_new = _dot(u_ref[ms, :], h_s[...])
                for tt in range(n_t):
                    st_score[tt, ms, :] = s_new[:, tt * LANE:(tt + 1) * LANE]


def _peer_body(x_ref, g_ref, sc_ref, sh_ref, gate_ref, wq_ref, sk_ref, u_ref, vt_ref, o_ref,
               h_s, q_s, r0_s, e0_s, r0c_s, e0c_s, n1_s, e1_s, top0_s, top1_s, cand_s,
               st0_s, st1_s, wt0_s, wt1_s, acc_s, *, n_eb):
    e = pl.program_id(1)
    assert u_ref.shape[0] == SUBLANES * P_NKEYS
    assert n_eb >= 2

    @pl.when(e == 0)
    def _():
        _peer_select(x_ref, g_ref, sc_ref, sh_ref, wq_ref, sk_ref,
                     h_s, q_s, r0_s, e0_s, n1_s, e1_s, top0_s, top1_s, cand_s)
        acc_s[...] = jnp.zeros_like(acc_s)

    gate_block = jnp.clip(e - 1, 0, n_eb - 1)
    group = pl.ds(pl.multiple_of(gate_block * SUBLANES, SUBLANES), SUBLANES)
    st = (st0_s, st1_s)
    wt = (wt0_s, wt1_s)

    def step(cond, slot, **parts):
        @pl.when(cond)
        def _():
            _peer_step(group, u_ref, vt_ref, h_s, r0_s, e0_s, r0c_s, e0c_s, n1_s, e1_s,
                       st[slot], st[1 - slot], wt[1 - slot], wt[slot], acc_s, **parts)

    step(e == 0, 0, do_acc=False, do_gate=False, do_score=True)
    step(e == 1, 1, do_acc=False, do_gate=True, do_score=True)
    for slot in range(2):
        step((e >= 2) & (e < n_eb) & (e % 2 == slot), slot, do_acc=True, do_gate=True, do_score=True)
    step(e == n_eb, n_eb % 2, do_acc=True, do_gate=True, do_score=False)
    step(e == n_eb + 1, (n_eb + 1) % 2, do_acc=True, do_gate=False, do_score=False)

    @pl.when(e == n_eb + 1)
    def _():
        o_ref[...] = x_ref[...] + gate_ref[0] * acc_s[...].T


def _peer(lay, x, g, mod, wq, sk2, u, vt):
    tn, en = PEER_TOKENS, PEER_EXPERTS
    n_eb = u.shape[0] // en
    assert vt.shape == (n_eb, D_MODEL, en)
    grid = (lay.nt // tn, n_eb + 2)
    head_scr = pltpu.VMEM((P_HEADS, P_NKEYS, tn), F32)
    rows_scr = pltpu.VMEM((P_HEADS, SUBLANES, tn), jnp.uint32)
    tile_scr = pltpu.VMEM((P_HEADS, tn // LANE, P_NKEYS, LANE), BF16)
    score_scr = pltpu.VMEM((tn // LANE, en, LANE), F32)
    gated_scr = pltpu.VMEM((en, tn), BF16)
    return pl.pallas_call(
        functools.partial(_peer_body, n_eb=n_eb),
        grid=grid,
        in_specs=[pl.BlockSpec((tn, D_MODEL), lambda i, e: (i, 0)),
                  pl.BlockSpec((1, D_MODEL), lambda i, e: (0, 0)),
                  _mod_spec(lay, tn, 4), _mod_spec(lay, tn, 3), _mod_spec(lay, tn, 5),
                  pl.BlockSpec((D_MODEL, P_HEADS * LANE), lambda i, e: (0, 0)),
                  pl.BlockSpec((P_HEADS, 2 * P_NKEYS, LANE), lambda i, e: (0, 0, 0)),
                  pl.BlockSpec((en, D_MODEL), lambda i, e: (jnp.minimum(e, n_eb - 1), 0)),
                  pl.BlockSpec((None, D_MODEL, en), lambda i, e: (jnp.clip(e - 2, 0, n_eb - 1), 0, 0))],
        out_specs=pl.BlockSpec((tn, D_MODEL), lambda i, e: (i, 0)),
        out_shape=jax.ShapeDtypeStruct((lay.nt, D_MODEL), F32),
        scratch_shapes=[pltpu.VMEM((D_MODEL, tn), BF16),
                        pltpu.VMEM((tn, P_HEADS * LANE), F32),
                        head_scr, head_scr, rows_scr, rows_scr, tile_scr, tile_scr,
                        pltpu.VMEM((P_TOPK, LANE), F32),
                        pltpu.VMEM((P_TOPK, LANE), F32),
                        pltpu.VMEM((_CAND_ROWS, LANE), F32),
                        score_scr, score_scr, gated_scr, gated_scr,
                        pltpu.VMEM((D_MODEL, tn), F32)],
        compiler_params=_params(2),
        name="peer",
    )(x, g.reshape(1, D_MODEL), mod, mod, mod, wq, sk2, u, vt)


def _expert_blocks_t(v):
    n_exp, d = v.shape
    return jnp.swapaxes(v.reshape(n_exp // PEER_EXPERTS, PEER_EXPERTS, d), 1, 2).astype(BF16)


def _sub_key_blocks(sub_keys):
    z = jnp.zeros_like(sub_keys[:, 0])
    top = jnp.concatenate([sub_keys[:, 0], z], axis=-1)
    bot = jnp.concatenate([z, sub_keys[:, 1]], axis=-1)
    return jnp.concatenate([top, bot], axis=1)


def _lambda_init(layer):
    return 0.8 - 0.6 * math.exp(-0.3 * layer)


def _even_w_in(w):
    aq, ak, av, bq, bk, bv, br, bg = jnp.split(w, [512, 640, 768, 1024, 1280, 1792, 2304], axis=1)
    pad = jnp.zeros((w.shape[0], E_WIDTH - E_BG - bg.shape[1]), w.dtype)
    return jnp.concatenate([aq, ak, av, bq, bv, br, bk, bg, pad], axis=1).astype(BF16)


def kernel(x_prompt, x_sample, cache_a_k, cache_a_v, state_b_fwd, state_b_bwd, cache_c_k, cache_c_v, c, c_ctx, ada_w, ada_b, norm_mix_g, norm_ffn_g, e_w_in, e_w_out, a_q_norm, a_k_norm, a_sink, b_gate_w_f, b_gate_b_f, b_gate_w_b, b_gate_b_b, b_out_norm, o_w_in, o_w_out, c_q_norm, c_k_norm, c_lambda_q1, c_lambda_k1, c_lambda_q2, c_lambda_k2, c_out_norm, p_w_q, p_sub_keys, p_u, p_v):
    bc, tc, d = x_prompt.shape
    bl, tl, _ = x_sample.shape
    depth = ada_w.shape[0]
    lay = _Layout(bc, tc, bl, tl)
    n_mod_rows = 8 * ((bl + 1 + 7) // 8)
    cvecs = jnp.concatenate([c, c_ctx[None, :], jnp.zeros((n_mod_rows - bl - 1, d), F32)], axis=0)
    mods = _modulation(cvecs, ada_w, ada_b)
    cos, sin = _rope_tables(tl)
    x = jnp.concatenate([x_prompt.reshape(lay.nctx, d), x_sample.reshape(bl * tl, d)], axis=0)

    ak_l, av_l, sf_l, sb_l, ck_l, cv_l = [], [], [], [], [], []
    for l in range(depth):
        i = l // 2
        mod = mods[l].reshape(n_mod_rows, 1, N_MOD * d)
        if l % 2 == 0:
            p = _ln_proj(lay, x, norm_mix_g[l], mod, 0, _even_w_in(e_w_in[i]))
            gains = jnp.concatenate([jnp.tile(a_q_norm[i], A_HEADS), jnp.tile(a_k_norm[i], A_KV_HEADS)])
            qk = _qk_prep(lay, p, E_QK_WIDTH, gains.reshape(1, E_QK_WIDTH), cos, sin)
            sink = a_sink[i].reshape(A_HEADS)
            oa = jnp.concatenate([_attn_a_ctx(lay, qk, p, sink),
                                  _attn_a_lat(lay, qk, p, sink, cache_a_k, cache_a_v, i)], axis=0)
            zrow = jnp.zeros((LANE - 2 * B_GATE_RANK, 256), F32)
            zgate = jnp.zeros((B_GATE_RANK, 256), F32)
            gw_f = jnp.concatenate([b_gate_w_f[i], zgate, zrow], axis=0)
            gw_b = jnp.concatenate([zgate, b_gate_w_b[i], zrow], axis=0)
            o_f, s_f = _gla(lay, p, gw_f, b_gate_b_f[i], _state_in(state_b_fwd[:, i]), False)
            ob, s_b = _gla(lay, p, gw_b, b_gate_b_b[i], _state_in(state_b_bwd[:, i]), True,
                           o_fwd=o_f, b_on=b_out_norm[i])
            w_out = e_w_out[i].astype(BF16)
            x = _linear_res(lay, [oa, ob], [w_out[:512], w_out[512:]], mod, 2, x)
            ak_l.append(qk[:lay.nctx, E_AK:E_AK + LANE].reshape(bc, tc, A_KV_HEADS, HEAD_DIM))
            av_l.append(p[:lay.nctx, E_AV:E_AV + LANE].reshape(bc, tc, A_KV_HEADS, HEAD_DIM))
            sf_l.append(_state_out(s_f[:bc]))
            sb_l.append(_state_out(s_b[:bc]))
        else:
            lam_init = _lambda_init(l)
            p = _ln_proj(lay, x, norm_mix_g[l], mod, 0, o_w_in[i].astype(BF16))
            gains = jnp.concatenate([jnp.tile(c_q_norm[i], 2 * C_HEADS), jnp.tile(c_k_norm[i], 2 * C_HEADS)])
            qk = _qk_prep(lay, p, O_QK_WIDTH, gains.reshape(1, O_QK_WIDTH), cos, sin)
            lamv = jnp.stack([c_lambda_q1[i], c_lambda_k1[i], c_lambda_q2[i], c_lambda_k2[i]])
            lamv = jnp.pad(lamv, ((0, 4), (0, LANE - HEAD_DIM)))
            o_ctx = _diff_attn(qk, p, lamv, c_out_norm[i], lam_init, bc, tc, 0, tc)
            o_lat = _diff_attn(qk, p, lamv, c_out_norm[i], lam_init, bl, tl, lay.nctx, DIFF_Q_BLOCK,
                               cache=(cache_c_k, cache_c_v, i))
            o = jnp.concatenate([o_ctx, o_lat], axis=0)
            x = _linear_res(lay, [o], [o_w_out[i].astype(BF16)], mod, 2, x)
            ck_l.append(qk[:lay.nctx, 1024:2048].reshape(bc, tc, C_HEADS, 2, HEAD_DIM))
            cv_l.append(p[:lay.nctx, 2048:3072].reshape(bc, tc, C_HEADS, C_DV))
        x = _peer(lay, x, norm_ffn_g[l], mod, p_w_q[l].astype(BF16), _sub_key_blocks(p_sub_keys[l]),
                  p_u[l].astype(BF16), _expert_blocks_t(p_v[l]))

    return (x[:lay.nctx].reshape(bc, tc, d), x[lay.nctx:].reshape(bl, tl, d),
            jnp.stack(ak_l, axis=1), jnp.stack(av_l, axis=1), jnp.stack(sf_l, axis=1), jnp.stack(sb_l, axis=1),
            jnp.stack(ck_l, axis=1), jnp.stack(cv_l, axis=1))
```
